```python
import math
import jax, jax.numpy as jnp
from jax import lax
import numpy as np

D_MODEL = 1024
BATCH = 4
SEQ = 4096
DEPTH = 2

HEAD_DIM = 64
SB_HEADS = D_MODEL // HEAD_DIM
SW_Q_HEADS = D_MODEL // HEAD_DIM
SW_KV_HEADS = 4
WINDOW = 128
BLOCK = 128
PEER_HEADS = 8
PEER_KEYS = 128
PEER_N_EXPERTS = PEER_KEYS * PEER_KEYS
PEER_QDIM = 256
PEER_HALF = PEER_QDIM // 2
PEER_TOPK = 16
PEER_CHUNK = 128
RMS_EPS = 1e-6
N_SB = (DEPTH + 1) // 2
N_SW = DEPTH // 2

kernel_name = 'hybrid_stickbreak_swa_sink_peer'


def rmsnorm(x, gain):
    xf = x.astype(jnp.float32)
    y = xf * lax.rsqrt(jnp.mean(xf * xf, axis=-1, keepdims=True) + RMS_EPS)
    return (y * gain.astype(jnp.float32)).astype(x.dtype)


def alibi_slopes(n_heads):
    return jnp.asarray(2.0 ** (-8.0 * np.arange(1, n_heads + 1) / n_heads), dtype=jnp.float32)


def stick_breaking_attention(x, w_qkv, w_o):
    B, S, _ = x.shape
    qkv = x @ w_qkv
    q, k, v = jnp.split(qkv, 3, axis=-1)
    to_heads = lambda a: a.reshape(B, S, SB_HEADS, HEAD_DIM).transpose(0, 2, 1, 3)
    q, k, v = to_heads(q), to_heads(k), to_heads(v)
    scale = HEAD_DIM ** -0.5
    outs = []
    for n in range(S // BLOCK):
        t0 = n * BLOCK
        end = t0 + BLOCK
        z = jnp.einsum('bhtd,bhsd->bhts', q[:, :, t0:end], k[:, :, :end]).astype(jnp.float32) * scale
        t_pos = t0 + jnp.arange(BLOCK)[:, None]
        s_pos = jnp.arange(end)[None, :]
        causal = s_pos < t_pos
        log_fail = jnp.where(causal, jax.nn.log_sigmoid(-z), 0.0)
        suffix = lax.cumsum(log_fail, axis=3, reverse=True) - log_fail
        log_a = jax.nn.log_sigmoid(z) + suffix
        a = jnp.where(causal, jnp.exp(log_a), 0.0)
        outs.append(jnp.einsum('bhts,bhsd->bhtd', a.astype(v.dtype), v[:, :, :end]))
    o = jnp.concatenate(outs, axis=2)
    o = o.transpose(0, 2, 1, 3).reshape(B, S, SB_HEADS * HEAD_DIM)
    return o @ w_o


def sliding_window_attention(x, w_qkv, q_gain, k_gain, sinks, w_o):
    B, S, _ = x.shape
    G = SW_Q_HEADS // SW_KV_HEADS
    nb = S // BLOCK
    qkv = x @ w_qkv
    q, k, v = jnp.split(qkv, [SW_Q_HEADS * HEAD_DIM, (SW_Q_HEADS + SW_KV_HEADS) * HEAD_DIM], axis=-1)
    q = rmsnorm(q.reshape(B, S, SW_KV_HEADS, G, HEAD_DIM), q_gain)
    k = rmsnorm(k.reshape(B, S, SW_KV_HEADS, HEAD_DIM), k_gain)
    v = v.reshape(B, S, SW_KV_HEADS, HEAD_DIM)
    qb = q.reshape(B, nb, BLOCK, SW_KV_HEADS, G, HEAD_DIM)

    def band(a):
        a = jnp.pad(a, ((0, 0), (BLOCK, 0), (0, 0), (0, 0)))
        a = a.reshape(B, nb + 1, BLOCK, SW_KV_HEADS, HEAD_DIM)
        return jnp.concatenate([a[:, :-1], a[:, 1:]], axis=2)

    kb, vb = band(k), band(v)
    scores = jnp.einsum('bnikgd,bnjkd->bnkgij', qb, kb).astype(jnp.float32) * (HEAD_DIM ** -0.5)
    i = jnp.arange(BLOCK)[:, None]
    j = jnp.arange(2 * BLOCK)[None, :]
    dist = BLOCK + i - j
    blk = jnp.arange(nb)[:, None, None]
    valid = (dist >= 0) & (dist < WINDOW) & (blk * BLOCK + j - BLOCK >= 0)
    slopes = alibi_slopes(SW_Q_HEADS).reshape(SW_KV_HEADS, G)
    scores = scores - slopes[:, :, None, None] * dist.astype(jnp.float32)
    scores = jnp.where(valid[None, :, None, None], scores, -jnp.inf)
    sink = sinks.astype(jnp.float32).reshape(SW_KV_HEADS, G)
    m = jnp.maximum(jnp.max(scores, axis=-1), sink[:, :, None])
    p = jnp.exp(scores - m[..., None])
    denom = jnp.sum(p, axis=-1) + jnp.exp(sink[:, :, None] - m)
    probs = (p / denom[..., None]).astype(vb.dtype)
    o = jnp.einsum('bnkgij,bnjkd->bnikgd', probs, vb)
    o = o.reshape(B, S, SW_Q_HEADS * HEAD_DIM)
    return o @ w_o


def peer(x, w_query, sub_keys, expert_u, expert_v):
    B, S, D = x.shape
    T = B * S
    xt = x.reshape(T, D)
    q = (xt @ w_query).reshape(T, PEER_HEADS, 2, PEER_HALF)
    sc = jnp.einsum('thpc,hpnc->thpn', q, sub_keys).astype(jnp.float32)
    top_s, top_i = lax.top_k(sc, PEER_TOPK)
    cand = top_s[:, :, 0, :, None] + top_s[:, :, 1, None, :]
    cand_s, cand_i = lax.top_k(cand.reshape(T, PEER_HEADS, PEER_TOPK * PEER_TOPK), PEER_TOPK)
    i1 = jnp.take_along_axis(top_i[:, :, 0], cand_i // PEER_TOPK, axis=-1)
    i2 = jnp.take_along_axis(top_i[:, :, 1], cand_i % PEER_TOPK, axis=-1)
    expert = i1 * PEER_KEYS + i2
    gate = jax.nn.softmax(cand_s, axis=-1)

    def chunk(args):
        xc, ec, gc = args
        h = jnp.einsum('cd,chkd->chk', xc, expert_u[ec])
        coef = (gc * jax.nn.gelu(h.astype(jnp.float32))).astype(xc.dtype)
        return jnp.einsum('chk,chkd->cd', coef, expert_v[ec])

    nc = T // PEER_CHUNK
    out = lax.map(chunk, (xt.reshape(nc, PEER_CHUNK, D),
                          expert.reshape(nc, PEER_CHUNK, PEER_HEADS, PEER_TOPK),
                          gate.reshape(nc, PEER_CHUNK, PEER_HEADS, PEER_TOPK)))
    return out.reshape(B, S, D)


def setup_inputs(seed: int = 0) -> dict:
    key = jax.random.key(seed)
    ks = jax.random.split(key, 16)
    nrm = lambda k, shape, s: jax.random.normal(k, shape, jnp.float32) * s
    gain = lambda k, shape: 1.0 + 0.02 * jax.random.normal(k, shape, jnp.float32)
    d_inv = D_MODEL ** -0.5
    sw_cols = (SW_Q_HEADS + 2 * SW_KV_HEADS) * HEAD_DIM
    return {
        'x': jax.random.normal(ks[0], (BATCH, SEQ, D_MODEL), jnp.float32),
        'sb_norm': gain(ks[1], (N_SB, D_MODEL)),
        'sb_w_qkv': nrm(ks[2], (N_SB, D_MODEL, 3 * SB_HEADS * HEAD_DIM), d_inv),
        'sb_w_o': nrm(ks[3], (N_SB, SB_HEADS * HEAD_DIM, D_MODEL), d_inv),
        'sw_norm': gain(ks[4], (N_SW, D_MODEL)),
        'sw_w_qkv': nrm(ks[5], (N_SW, D_MODEL, sw_cols), d_inv),
        'sw_q_gain': gain(ks[6], (N_SW, HEAD_DIM)),
        'sw_k_gain': gain(ks[7], (N_SW, HEAD_DIM)),
        'sw_sinks': nrm(ks[8], (N_SW, SW_Q_HEADS), 0.5),
        'sw_w_o': nrm(ks[9], (N_SW, SW_Q_HEADS * HEAD_DIM, D_MODEL), d_inv),
        'ffn_norm': gain(ks[10], (DEPTH, D_MODEL)),
        'peer_w_query': nrm(ks[11], (DEPTH, D_MODEL, PEER_HEADS * PEER_QDIM), d_inv),
        'peer_sub_keys': nrm(ks[12], (DEPTH, PEER_HEADS, 2, PEER_KEYS, PEER_HALF), PEER_HALF ** -0.5),
        'peer_u': nrm(ks[13], (DEPTH, PEER_N_EXPERTS, D_MODEL), d_inv),
        'peer_v': nrm(ks[14], (DEPTH, PEER_N_EXPERTS, D_MODEL), PEER_HEADS ** -0.5),
    }


def reference(x, sb_norm, sb_w_qkv, sb_w_o, sw_norm, sw_w_qkv, sw_q_gain, sw_k_gain, sw_sinks, sw_w_o,
              ffn_norm, peer_w_query, peer_sub_keys, peer_u, peer_v):
    h = x
    for i in range(DEPTH):
        j = i // 2
        if i % 2 == 0:
            h = h + stick_breaking_attention(rmsnorm(h, sb_norm[j]), sb_w_qkv[j], sb_w_o[j])
        else:
            h = h + sliding_window_attention(rmsnorm(h, sw_norm[j]), sw_w_qkv[j], sw_q_gain[j],
                                             sw_k_gain[j], sw_sinks[j], sw_w_o[j])
        h = h + peer(rmsnorm(h, ffn_norm[i]), peer_w_query[i], peer_sub_keys[i], peer_u[i], peer_v[i])
    return h
```

```python
import functools

import numpy as np
import jax
import jax.numpy as jnp
from jax import lax
from jax.experimental import pallas as pl
from jax.experimental.pallas import tpu as pltpu

F32 = jnp.float32
BF16 = jnp.bfloat16

D_MODEL = 1024
HEAD_DIM = 64
SB_HEADS = 16
SW_Q_HEADS = 16
SW_KV_HEADS = 4
BLOCK = 128
LANES = 128
PEER_HEADS = 8
PEER_KEYS = 128
PEER_HALF = 128
PEER_TOPK = 16
RMS_EPS = 1e-6
ATTN_SCALE = HEAD_DIM ** -0.5
SB_DEAD_LOG = -110.0
NEG_INF = float("-inf")
VMEM_LIMIT = 48 * 1024 * 1024

_NT = (((1,), (1,)), ((), ()))


def _params(*sem):
    return pltpu.CompilerParams(dimension_semantics=sem, vmem_limit_bytes=VMEM_LIMIT)


def _rms_scale(x):
    return lax.rsqrt(jnp.mean(x * x, axis=-1, keepdims=True) + RMS_EPS)


def _norm_matmul_kernel(x_ref, g_ref, w_ref, o_ref):
    x = x_ref[...]
    xn = (x * _rms_scale(x) * g_ref[...]).astype(BF16)
    o_ref[...] = jnp.dot(xn, w_ref[...], preferred_element_type=F32).astype(o_ref.dtype)


def norm_matmul(x, g, w, out_dtype, tm=512):
    t, d = x.shape
    n = w.shape[1]
    tm = min(tm, t)
    return pl.pallas_call(
        _norm_matmul_kernel,
        grid=(t // tm,),
        in_specs=[pl.BlockSpec((tm, d), lambda i: (i, 0)),
                  pl.BlockSpec((1, d), lambda i: (0, 0)),
                  pl.BlockSpec((d, n), lambda i: (0, 0))],
        out_specs=pl.BlockSpec((tm, n), lambda i: (i, 0)),
        out_shape=jax.ShapeDtypeStruct((t, n), out_dtype),
        compiler_params=_params("parallel"),
        name="norm_matmul",
    )(x, g.reshape(1, d), w)


def _sb_kernel(q_ref, k_ref, v_ref, tri_ref, o_ref):
    i = pl.program_id(2)
    tq = BLOCK
    lane = lax.broadcasted_iota(jnp.int32, (tq, LANES), 1)
    row = lax.broadcasted_iota(jnp.int32, (tq, tq), 0)
    col = lax.broadcasted_iota(jnp.int32, (tq, tq), 1)
    causal = col < row
    q2 = (q_ref[0].astype(F32) * ATTN_SCALE).astype(BF16)
    tri = tri_ref[...]

    def block(qh, j, carry, acc, diag):
        start = pl.multiple_of(j * tq, tq)
        kj = k_ref[0, pl.ds(start, tq), :]
        vj = v_ref[0, pl.ds(start, tq), :]
        z = lax.dot_general(qh, kj, _NT, preferred_element_type=F32)
        lf = -(jnp.maximum(z, 0.0) + jnp.log1p(jnp.exp(-jnp.abs(z))))
        lfm = jnp.where(causal, lf, 0.0) if diag else lf
        hi = lfm.astype(BF16)
        lo = (lfm - hi.astype(F32)).astype(BF16)
        wt = jnp.dot(jnp.concatenate([hi, lo], axis=1), tri, preferred_element_type=F32)
        a = jnp.exp(z + lf + wt[:, :tq] + carry)
        if diag:
            a = jnp.where(causal, a, 0.0)
        acc = acc + jnp.dot(a.astype(BF16), vj, preferred_element_type=F32)
        return carry + wt[:, tq:], acc

    accs = []
    for hh in range(2):
        hmask = (lane < HEAD_DIM) if hh == 0 else (lane >= HEAD_DIM)
        qh = jnp.where(hmask, q2, jnp.zeros_like(q2))
        zero = jnp.zeros((tq, LANES), F32)
        carry0, acc0 = block(qh, i, zero, zero, True)

        def cond(st):
            j, _, _, m = st
            return jnp.logical_and(j >= 0, m > SB_DEAD_LOG)

        def body(st, qh=qh):
            j, carry, acc, _ = st
            carry, acc = block(qh, j, carry, acc, False)
            return j - 1, carry, acc, jnp.max(carry)

        st = lax.while_loop(cond, body, (i - 1, carry0, acc0, jnp.max(carry0)))
        accs.append(st[2])
    o_ref[0] = jnp.where(lane < HEAD_DIM, accs[0], accs[1]).astype(o_ref.dtype)


def _sb_tri():
    r = np.arange(2 * BLOCK)[:, None] % BLOCK
    c = np.arange(2 * BLOCK)[None, :]
    return jnp.asarray(np.where(c < BLOCK, r > c, True), dtype=BF16)


def sb_attention(qkv, batch, seq):
    npair = SB_HEADS * HEAD_DIM // LANES
    return pl.pallas_call(
        _sb_kernel,
        grid=(batch, npair, seq // BLOCK),
        in_specs=[pl.BlockSpec((1, BLOCK, LANES), lambda b, p, i: (b, i, p)),
                  pl.BlockSpec((1, seq, LANES), lambda b, p, i: (b, 0, npair + p)),
                  pl.BlockSpec((1, seq, LANES), lambda b, p, i: (b, 0, 2 * npair + p)),
                  pl.BlockSpec((2 * BLOCK, 2 * BLOCK), lambda b, p, i: (0, 0))],
        out_specs=pl.BlockSpec((1, BLOCK, LANES), lambda b, p, i: (b, i, p)),
        out_shape=jax.ShapeDtypeStruct((batch, seq, SB_HEADS * HEAD_DIM), BF16),
        compiler_params=_params("parallel", "parallel", "arbitrary"),
        name="sb_attention",
    )(qkv, qkv, qkv, _sb_tri())


def _alibi_slopes(n_heads):
    return [float(v) for v in np.asarray(
        2.0 ** (-8.0 * np.arange(1, n_heads + 1) / n_heads), dtype=np.float32)]


def _sw_kernel(sink_ref, q_ref, kp_ref, kc_ref, vp_ref, vc_ref, qg_ref, kg_ref, bd_ref, o_ref):
    n = pl.program_id(1)
    lane = lax.broadcasted_iota(jnp.int32, (BLOCK, LANES), 1)
    ii = lax.broadcasted_iota(jnp.int32, (BLOCK, 2 * BLOCK), 0)
    jj = lax.broadcasted_iota(jnp.int32, (BLOCK, 2 * BLOCK), 1)
    dist = BLOCK + ii - jj
    valid = (dist >= 0) & (dist < BLOCK) & jnp.logical_or(jj >= BLOCK, n > 0)
    distf = dist.astype(F32)
    slopes = _alibi_slopes(SW_Q_HEADS)
    bd = bd_ref[...]
    group = SW_Q_HEADS // SW_KV_HEADS
    for kk in range(SW_KV_HEADS):
        cs = slice(kk * LANES, (kk + 1) * LANES)
        k2 = jnp.concatenate([kp_ref[0, :, cs], kc_ref[0, :, cs]], axis=0)
        kn = (k2 * _rms_scale(k2) * kg_ref[...]).astype(BF16)
        v2 = jnp.concatenate([vp_ref[0, :, cs], vc_ref[0, :, cs]], axis=0).astype(BF16)
        for gp in range(group // 2):
            pair = kk * (group // 2) + gp
            qq = q_ref[0, :, pair * LANES:(pair + 1) * LANES]
            sq = qq * qq
            hi = sq.astype(BF16)
            lo = (sq - hi.astype(F32)).astype(BF16)
            ssq = (jnp.dot(hi, bd, preferred_element_type=F32)
                   + jnp.dot(lo, bd, preferred_element_type=F32))
            qn = qq * lax.rsqrt(ssq * (1.0 / HEAD_DIM) + RMS_EPS) * qg_ref[...]
            qn = (qn * ATTN_SCALE).astype(BF16)
            outs = []
            for e in range(2):
                head = 2 * pair + e
                hmask = (lane < HEAD_DIM) if e == 0 else (lane >= HEAD_DIM)
                qh = jnp.where(hmask, qn, jnp.zeros_like(qn))
                s = lax.dot_general(qh, kn, _NT, preferred_element_type=F32)
                s = jnp.where(valid, s - slopes[head] * distf, NEG_INF)
                sink = sink_ref[head]
                m = jnp.maximum(jnp.max(s, axis=-1, keepdims=True), sink)
                p = jnp.exp(s - m)
                denom = jnp.sum(p, axis=-1, keepdims=True) + jnp.exp(sink - m)
                o = jnp.dot(p.astype(BF16), v2, preferred_element_type=F32)
                outs.append(o / denom)
            o_ref[0, :, pair * LANES:(pair + 1) * LANES] = jnp.where(
                lane < HEAD_DIM, outs[0], outs[1]).astype(o_ref.dtype)


def sw_attention(qkv, sinks, q_gain, k_gain, batch, seq):
    nb = seq // BLOCK
    qw = SW_Q_HEADS * HEAD_DIM
    kw = SW_KV_HEADS * LANES
    bd = jnp.asarray(np.kron(np.eye(2), np.ones((HEAD_DIM, HEAD_DIM))), dtype=BF16)
    prev = lambda b, n: (b, jnp.maximum(n - 1, 0), qw // kw)
    cur = lambda b, n: (b, n, qw // kw)
    prev_v = lambda b, n: (b, jnp.maximum(n - 1, 0), qw // kw + 1)
    cur_v = lambda b, n: (b, n, qw // kw + 1)
    vec = lambda g: jnp.tile(g.astype(F32), 2).reshape(1, LANES)
    return pl.pallas_call(
        _sw_kernel,
        grid=(batch, nb),
        in_specs=[pl.BlockSpec(memory_space=pltpu.SMEM),
                  pl.BlockSpec((1, BLOCK, qw), lambda b, n: (b, n, 0)),
                  pl.BlockSpec((1, BLOCK, kw), prev),
                  pl.BlockSpec((1, BLOCK, kw), cur),
                  pl.BlockSpec((1, BLOCK, kw), prev_v),
                  pl.BlockSpec((1, BLOCK, kw), cur_v),
                  pl.BlockSpec((1, LANES), lambda b, n: (0, 0)),
                  pl.BlockSpec((1, LANES), lambda b, n: (0, 0)),
                  pl.BlockSpec((LANES, LANES), lambda b, n: (0, 0))],
        out_specs=pl.BlockSpec((1, BLOCK, qw), lambda b, n: (b, n, 0)),
        out_shape=jax.ShapeDtypeStruct((batch, seq, qw), BF16),
        compiler_params=_params("parallel", "arbitrary"),
        name="sw_attention",
    )(sinks.astype(F32), qkv, qkv, qkv, qkv, qkv, vec(q_gain), vec(k_gain), bd)


def _sw_qkv_weight(w):
    d = w.shape[0]
    qw = SW_Q_HEADS * HEAD_DIM
    kvw = SW_KV_HEADS * HEAD_DIM
    dup = lambda a: jnp.broadcast_to(
        a.reshape(d, SW_KV_HEADS, 1, HEAD_DIM), (d, SW_KV_HEADS, 2, HEAD_DIM)).reshape(d, 2 * kvw)
    return jnp.concatenate([w[:, :qw], dup(w[:, qw:qw + kvw]), dup(w[:, qw + kvw:])], axis=1)


def _proj_kernel(o_ref, wo_ref, h_ref, g_ref, wq_ref, hn_ref, xn_ref, q_ref):
    hn = h_ref[...] + jnp.dot(o_ref[...], wo_ref[...], preferred_element_type=F32)
    hn_ref[...] = hn
    xn = (hn * _rms_scale(hn) * g_ref[...]).astype(BF16)
    xn_ref[...] = xn
    q_ref[...] = jnp.dot(xn, wq_ref[...], preferred_element_type=F32).astype(BF16)


def proj_residual_query(o, wo, h, g, wq, tm=512):
    t, d = h.shape
    nq = wq.shape[1]
    tm = min(tm, t)
    row = lambda i: (i, 0)
    fixed = lambda i: (0, 0)
    return pl.pallas_call(
        _proj_kernel,
        grid=(t // tm,),
        in_specs=[pl.BlockSpec((tm, d), row), pl.BlockSpec((d, d), fixed),
                  pl.BlockSpec((tm, d), row), pl.BlockSpec((1, d), fixed),
                  pl.BlockSpec((d, nq), fixed)],
        out_specs=[pl.BlockSpec((tm, d), row), pl.BlockSpec((tm, d), row),
                   pl.BlockSpec((tm, nq), row)],
        out_shape=[jax.ShapeDtypeStruct((t, d), F32), jax.ShapeDtypeStruct((t, d), BF16),
                   jax.ShapeDtypeStruct((t, nq), BF16)],
        compiler_params=_params("parallel"),
        name="proj_residual_query",
    )(o, wo, h, g.reshape(1, d), wq)


def _cand_tables():
    k = PEER_TOPK
    flat = np.zeros((80,), np.float32)
    neg = np.zeros((80,), np.float32)
    for a in range(k):
        flat[a] = a * k
    for b in range(1, 8):
        for a in range(8):
            r = 16 + 8 * (b - 1) + a
            flat[r] = a * k + b
            if (a + 1) * (b + 1) > k:
                neg[r] = NEG_INF
    for b in range(8, k):
        flat[72 + b - 8] = b
    tile = lambda v: jnp.asarray(np.repeat(v[:, None], LANES, axis=1))
    return tile(flat), tile(neg)


def _topk_kernel(q_ref, sk_ref, flat_ref, neg_ref, ns_ref, e1_ref, r2_ref, e2_ref,
                 s_scr, tv_scr, ix_scr, cand_scr, sel_scr):
    tt = LANES
    nl = 2 * PEER_HEADS
    iota = lax.broadcasted_iota(jnp.int32, (PEER_KEYS, tt), 0).astype(F32)
    big = float(4 * PEER_KEYS * PEER_KEYS)

    for l in range(nl):
        s_scr[l] = lax.dot_general(sk_ref[l], q_ref[:, l * PEER_HALF:(l + 1) * PEER_HALF],
                                   _NT, preferred_element_type=F32)

    def l1_body(r, c):
        for l in range(nl):
            s = s_scr[l]
            m = jnp.max(s, axis=0, keepdims=True)
            ix = jnp.min(jnp.where(s == m, iota, big), axis=0, keepdims=True)
            s_scr[l] = jnp.where(iota == ix, NEG_INF, s)
            tv_scr[l, pl.ds(r, 1), :] = m
            ix_scr[l, pl.ds(r, 1), :] = ix
        return c

    lax.fori_loop(0, PEER_TOPK, l1_body, 0)

    flat = flat_ref[...]
    for h in range(PEER_HEADS):
        tv1 = tv_scr[2 * h]
        tv2 = tv_scr[2 * h + 1]
        parts = [tv1 + tv2[0:1, :]]
        for b in range(1, 8):
            parts.append(tv1[0:8, :] + tv2[b:b + 1, :])
        parts.append(tv1[0:1, :] + tv2[8:16, :])
        cand_scr[h] = jnp.concatenate(parts, axis=0) + neg_ref[...]
        sel_scr[h] = jnp.zeros((80, tt), F32)

    def l2_body(r, c):
        for h in range(PEER_HEADS):
            cd = cand_scr[h]
            m = jnp.max(cd, axis=0, keepdims=True)
            ix = jnp.min(jnp.where(cd == m, flat, big), axis=0, keepdims=True)
            hit = flat == ix
            cand_scr[h] = jnp.where(hit, NEG_INF, cd)
            sel_scr[h] = jnp.where(hit, 1.0, sel_scr[h])
        return c

    lax.fori_loop(0, PEER_TOPK, l2_body, 0)

    row8 = lax.broadcasted_iota(jnp.int32, (8, tt), 0)
    for h in range(PEER_HEADS):
        tv1 = tv_scr[2 * h]
        tv2 = tv_scr[2 * h + 1]
        ix1 = ix_scr[2 * h]
        ix2 = ix_scr[2 * h + 1]
        sel = sel_scr[h]
        x1 = jnp.exp(tv1 - tv1[0:1, :])
        x2 = jnp.exp(tv2 - tv2[0:1, :])
        eparts = [x1 * x2[0:1, :]]
        for b in range(1, 8):
            eparts.append(x1[0:8, :] * x2[b:b + 1, :])
        eparts.append(x1[0:1, :] * x2[8:16, :])
        z = jnp.sum(sel * jnp.concatenate(eparts, axis=0), axis=0, keepdims=True)
        n_lo = sel[0:8, :]
        for b in range(1, 8):
            n_lo = n_lo + sel[16 + 8 * (b - 1):24 + 8 * (b - 1), :]
        n_lo = n_lo + jnp.where(row8 == 0, jnp.sum(sel[72:80, :], axis=0, keepdims=True), 0.0)
        n_a = jnp.concatenate([n_lo, sel[8:16, :]], axis=0)
        s1 = lax.dot_general(sk_ref[2 * h], q_ref[:, (2 * h) * PEER_HALF:(2 * h + 1) * PEER_HALF],
                             _NT, preferred_element_type=F32)
        s2 = lax.dot_general(sk_ref[2 * h + 1],
                             q_ref[:, (2 * h + 1) * PEER_HALF:(2 * h + 2) * PEER_HALF],
                             _NT, preferred_element_type=F32)
        ns = jnp.zeros((PEER_KEYS, tt), F32)
        r2 = jnp.full((PEER_KEYS, tt), float(PEER_TOPK), F32)
        for a in range(PEER_TOPK):
            ns = jnp.where(iota == ix1[a:a + 1, :], n_a[a:a + 1, :], ns)
            r2 = jnp.where(iota == ix2[a:a + 1, :], float(a), r2)
        ns_ref[h] = ns
        r2_ref[h] = r2
        e1_ref[h] = jnp.exp(s1 - tv1[0:1, :]) / z
        e2_ref[h] = jnp.exp(s2 - tv2[0:1, :])


def peer_topk(q, sub_keys):
    t = q.shape[0]
    tt = LANES
    nl = 2 * PEER_HEADS
    flat, neg = _cand_tables()
    out = jax.ShapeDtypeStruct((PEER_HEADS, PEER_KEYS, t), F32)
    ospec = pl.BlockSpec((PEER_HEADS, PEER_KEYS, tt), lambda i: (0, 0, i))
    return pl.pallas_call(
        _topk_kernel,
        grid=(t // tt,),
        in_specs=[pl.BlockSpec((tt, nl * PEER_HALF), lambda i: (i, 0)),
                  pl.BlockSpec((nl, PEER_KEYS, PEER_HALF), lambda i: (0, 0, 0)),
                  pl.BlockSpec((80, tt), lambda i: (0, 0)),
                  pl.BlockSpec((80, tt), lambda i: (0, 0))],
        out_specs=[ospec, ospec, ospec, ospec],
        out_shape=[out, out, out, out],
        scratch_shapes=[pltpu.VMEM((nl, PEER_KEYS, tt), F32),
                        pltpu.VMEM((nl, PEER_TOPK, tt), F32),
                        pltpu.VMEM((nl, PEER_TOPK, tt), F32),
                        pltpu.VMEM((PEER_HEADS, 80, tt), F32),
                        pltpu.VMEM((PEER_HEADS, 80, tt), F32)],
        compiler_params=_params("parallel"),
        name="peer_topk",
    )(q, sub_keys, flat, neg)


def _gelu_tanh(x):
    c = float(np.sqrt(2.0 / np.pi))
    return 0.5 * x * (1.0 + jnp.tanh(c * (x + 0.044715 * (x * x * x))))


def _peer_kernel(xn_ref, u_ref, vt_ref, ns_ref, e1_ref, r2_ref, e2_ref, h_ref, o_ref,
                 acc_ref, coef_ref, *, et, tt):
    j = pl.program_id(1)
    rows = et // PEER_KEYS

    @pl.when(j == 0)
    def _():
        acc_ref[...] = jnp.zeros_like(acc_ref)

    ht = lax.dot_general(u_ref[...], xn_ref[...], _NT, preferred_element_type=F32)
    base = pl.multiple_of(j * rows, rows)
    for c in range(tt // LANES):
        cs = slice(c * LANES, (c + 1) * LANES)
        ns_rows = [ns_ref[h, pl.ds(base, rows), cs] for h in range(PEER_HEADS)]
        e1_rows = [e1_ref[h, pl.ds(base, rows), cs] for h in range(PEER_HEADS)]
        for r in range(rows):
            g = jnp.zeros((PEER_KEYS, LANES), F32)
            for h in range(PEER_HEADS):
                ns = ns_rows[h][r:r + 1, :]
                e1 = e1_rows[h][r:r + 1, :]
                g = g + jnp.where(r2_ref[h, :, cs] < ns, e2_ref[h, :, cs], 0.0) * e1
            hrc = ht[r * PEER_KEYS:(r + 1) * PEER_KEYS, cs]
            coef_ref[r * PEER_KEYS:(r + 1) * PEER_KEYS, cs] = (g * _gelu_tanh(hrc)).astype(BF16)
    acc_ref[...] += jnp.dot(vt_ref[...], coef_ref[...], preferred_element_type=F32)

    @pl.when(j == pl.num_programs(1) - 1)
    def _():
        o_ref[...] = h_ref[...] + acc_ref[...].T


def peer_mix(xn, u, vt, ns, e1, r2, e2, h, et=1024, tt=512):
    t, d = h.shape
    ne = u.shape[0]
    tt = min(tt, t)
    gspec = pl.BlockSpec((PEER_HEADS, PEER_KEYS, tt), lambda i, j: (0, 0, i))
    return pl.pallas_call(
        functools.partial(_peer_kernel, et=et, tt=tt),
        grid=(t // tt, ne // et),
        in_specs=[pl.BlockSpec((tt, d), lambda i, j: (i, 0)),
                  pl.BlockSpec((et, d), lambda i, j: (j, 0)),
                  pl.BlockSpec((d, et), lambda i, j: (0, j)),
                  gspec, gspec, gspec, gspec,
                  pl.BlockSpec((tt, d), lambda i, j: (i, 0))],
        out_specs=pl.BlockSpec((tt, d), lambda i, j: (i, 0)),
        out_shape=jax.ShapeDtypeStruct((t, d), F32),
        scratch_shapes=[pltpu.VMEM((d, tt), F32), pltpu.VMEM((et, tt), BF16)],
        compiler_params=_params("parallel", "arbitrary"),
        name="peer_mix",
    )(xn, u, vt, ns, e1, r2, e2, h)


def _peer_block(o, wo, h, g, wq, sub_keys, u, v):
    hn, xn, q = proj_residual_query(o, wo.astype(BF16), h, g, wq.astype(BF16))
    sk = sub_keys.reshape(2 * PEER_HEADS, PEER_KEYS, PEER_HALF).astype(BF16)
    ns, e1, r2, e2 = peer_topk(q, sk)
    return peer_mix(xn, u.astype(BF16), v.T.astype(BF16), ns, e1, r2, e2, hn)


def kernel(x, sb_norm, sb_w_qkv, sb_w_o, sw_norm, sw_w_qkv, sw_q_gain, sw_k_gain, sw_sinks, sw_w_o,
           ffn_norm, peer_w_query, peer_sub_keys, peer_u, peer_v):
    batch, seq, d = x.shape
    t = batch * seq
    h = x.reshape(t, d)

    qkv = norm_matmul(h, sb_norm[0], sb_w_qkv[0].astype(BF16), BF16)
    o = sb_attention(qkv.reshape(batch, seq, -1), batch, seq).reshape(t, -1)
    h = _peer_block(o, sb_w_o[0], h, ffn_norm[0], peer_w_query[0], peer_sub_keys[0],
                    peer_u[0], peer_v[0])

    qkv = norm_matmul(h, sw_norm[0], _sw_qkv_weight(sw_w_qkv[0]).astype(BF16), F32)
    o = sw_attention(qkv.reshape(batch, seq, -1), sw_sinks[0], sw_q_gain[0], sw_k_gain[0],
                     batch, seq).reshape(t, -1)
    h = _peer_block(o, sw_w_o[0], h, ffn_norm[1], peer_w_query[1], peer_sub_keys[1],
                    peer_u[1], peer_v[1])
    return h.reshape(batch, seq, d)
```

```python
import functools

import numpy as np
import jax
import jax.numpy as jnp
from jax import lax
from jax.experimental import pallas as pl
from jax.experimental.pallas import tpu as pltpu

F32 = jnp.float32
BF16 = jnp.bfloat16

D_MODEL = 1024
HEAD_DIM = 64
SB_HEADS = 16
SW_Q_HEADS = 16
SW_KV_HEADS = 4
BLOCK = 128
LANES = 128
PEER_HEADS = 8
PEER_KEYS = 128
PEER_HALF = 128
PEER_TOPK = 16
RMS_EPS = 1e-6
ATTN_SCALE = HEAD_DIM ** -0.5
SB_DEAD_LOG = -110.0
NEG_INF = float("-inf")
VMEM_LIMIT = 48 * 1024 * 1024

_NT = (((1,), (1,)), ((), ()))


def _params(*sem, flags=None):
    return pltpu.CompilerParams(dimension_semantics=sem, vmem_limit_bytes=VMEM_LIMIT, flags=flags)


def _rms_scale(x):
    return lax.rsqrt(jnp.mean(x * x, axis=-1, keepdims=True) + RMS_EPS)


def _norm_matmul_kernel(x_ref, g_ref, w_ref, o_ref):
    x = x_ref[...]
    xn = (x * _rms_scale(x) * g_ref[...]).astype(BF16)
    o_ref[...] = jnp.dot(xn, w_ref[...], preferred_element_type=F32).astype(o_ref.dtype)


def norm_matmul(x, g, w, out_dtype, tm=512):
    t, d = x.shape
    n = w.shape[1]
    tm = min(tm, t)
    return pl.pallas_call(
        _norm_matmul_kernel,
        grid=(t // tm,),
        in_specs=[pl.BlockSpec((tm, d), lambda i: (i, 0)),
                  pl.BlockSpec((1, d), lambda i: (0, 0)),
                  pl.BlockSpec((d, n), lambda i: (0, 0))],
        out_specs=pl.BlockSpec((tm, n), lambda i: (i, 0)),
        out_shape=jax.ShapeDtypeStruct((t, n), out_dtype),
        compiler_params=_params("parallel"),
        name="norm_matmul",
    )(x, g.reshape(1, d), w)


def _sb_kernel(q_ref, k_ref, v_ref, tri_ref, o_ref):
    i = pl.program_id(2)
    tq = BLOCK
    lane = lax.broadcasted_iota(jnp.int32, (tq, LANES), 1)
    row = lax.broadcasted_iota(jnp.int32, (tq, tq), 0)
    col = lax.broadcasted_iota(jnp.int32, (tq, tq), 1)
    causal = col < row
    tri = tri_ref[...]

    def block(qh, j, carry, acc, diag):
        start = j * tq if isinstance(j, int) else pl.multiple_of(j * tq, tq)
        kj = k_ref[0, pl.ds(start, tq), :]
        vj = v_ref[0, pl.ds(start, tq), :]
        z = lax.dot_general(qh, kj, _NT, preferred_element_type=F32)
        lf = -(jnp.maximum(z, 0.0) + jnp.log1p(jnp.exp(-jnp.abs(z))))
        lfm = jnp.where(causal, lf, 0.0) if diag else lf
        hi = lfm.astype(BF16)
        lo = (lfm - hi.astype(F32)).astype(BF16)
        wt = jnp.dot(jnp.concatenate([hi, lo], axis=1), tri, preferred_element_type=F32)
        a = jnp.exp(z + lf + wt[:, :tq] + carry)
        if diag:
            a = jnp.where(causal, a, 0.0)
        acc = acc + jnp.dot(a.astype(BF16), vj, preferred_element_type=F32)
        return carry + wt[:, tq:], acc

    qs = []
    for u in range(2):
        q2 = (q_ref[0, u * tq:(u + 1) * tq, :].astype(F32) * ATTN_SCALE).astype(BF16)
        for hh in range(2):
            hmask = (lane < HEAD_DIM) if hh == 0 else (lane >= HEAD_DIM)
            qs.append(jnp.where(hmask, q2, jnp.zeros_like(q2)))
    zero = jnp.zeros((tq, LANES), F32)
    carries, accs = [], []
    for c in range(4):
        carry, acc = block(qs[c], 2 * i + c // 2, zero, zero, True)
        carries.append(carry)
        accs.append(acc)

    def alive(cs):
        return jnp.max(jnp.maximum(jnp.maximum(cs[0], cs[1]), jnp.maximum(cs[2], cs[3])))

    def cond(st):
        k, _, _, m = st
        return jnp.logical_and(k < 2 * i, m > SB_DEAD_LOG)

    def body(st):
        k, cs, ac, _ = st
        cs, ac = list(cs), list(ac)
        for c in range(4):
            cs[c], ac[c] = block(qs[c], 2 * i + c // 2 - 1 - k, cs[c], ac[c], False)
        return k + 1, tuple(cs), tuple(ac), alive(cs)

    k_end, carries, accs, _ = lax.while_loop(
        cond, body, (jnp.int32(0), tuple(carries), tuple(accs), alive(carries)))

    def last(ops):
        cs, ac = ops
        return tuple(block(qs[c], 0, cs[c - 2], ac[c - 2], False)[1] for c in (2, 3))

    tail = lax.cond(k_end == 2 * i, last, lambda ops: ops[1], (carries[2:], accs[2:]))
    outs = (accs[0], accs[1], tail[0], tail[1])
    for u in range(2):
        o_ref[0, u * tq:(u + 1) * tq, :] = jnp.where(
            lane < HEAD_DIM, outs[2 * u], outs[2 * u + 1]).astype(o_ref.dtype)


def _sb_tri():
    r = np.arange(2 * BLOCK)[:, None] % BLOCK
    c = np.arange(2 * BLOCK)[None, :]
    return jnp.asarray(np.where(c < BLOCK, r > c, True), dtype=BF16)


def sb_attention(qkv, batch, seq):
    npair = SB_HEADS * HEAD_DIM // LANES
    return pl.pallas_call(
        _sb_kernel,
        grid=(batch, npair, seq // (2 * BLOCK)),
        in_specs=[pl.BlockSpec((1, 2 * BLOCK, LANES), lambda b, p, i: (b, i, p)),
                  pl.BlockSpec((1, seq, LANES), lambda b, p, i: (b, 0, npair + p)),
                  pl.BlockSpec((1, seq, LANES), lambda b, p, i: (b, 0, 2 * npair + p)),
                  pl.BlockSpec((2 * BLOCK, 2 * BLOCK), lambda b, p, i: (0, 0))],
        out_specs=pl.BlockSpec((1, 2 * BLOCK, LANES), lambda b, p, i: (b, i, p)),
        out_shape=jax.ShapeDtypeStruct((batch, seq, SB_HEADS * HEAD_DIM), BF16),
        compiler_params=_params("parallel", "parallel", "arbitrary"),
        name="sb_attention",
    )(qkv, qkv, qkv, _sb_tri())


def _alibi_slopes(n_heads):
    return [float(v) for v in np.asarray(
        2.0 ** (-8.0 * np.arange(1, n_heads + 1) / n_heads), dtype=np.float32)]


def _sw_kernel(sink_ref, q_ref, kp_ref, kc_ref, vp_ref, vc_ref, qg_ref, kg_ref, bd_ref, o_ref):
    n = pl.program_id(1)
    lane = lax.broadcasted_iota(jnp.int32, (BLOCK, LANES), 1)
    ii = lax.broadcasted_iota(jnp.int32, (BLOCK, 2 * BLOCK), 0)
    jj = lax.broadcasted_iota(jnp.int32, (BLOCK, 2 * BLOCK), 1)
    dist = BLOCK + ii - jj
    valid = (dist >= 0) & (dist < BLOCK) & jnp.logical_or(jj >= BLOCK, n > 0)
    distf = dist.astype(F32)
    slopes = _alibi_slopes(SW_Q_HEADS)
    bd = bd_ref[...]
    group = SW_Q_HEADS // SW_KV_HEADS
    for kk in range(SW_KV_HEADS):
        cs = slice(kk * LANES, (kk + 1) * LANES)
        k2 = jnp.concatenate([kp_ref[0, :, cs], kc_ref[0, :, cs]], axis=0)
        kn = (k2 * _rms_scale(k2) * kg_ref[...]).astype(BF16)
        v2 = jnp.concatenate([vp_ref[0, :, cs], vc_ref[0, :, cs]], axis=0).astype(BF16)
        for gp in range(group // 2):
            pair = kk * (group // 2) + gp
            qq = q_ref[0, :, pair * LANES:(pair + 1) * LANES]
            sq = qq * qq
            hi = sq.astype(BF16)
            lo = (sq - hi.astype(F32)).astype(BF16)
            ssq = (jnp.dot(hi, bd, preferred_element_type=F32)
                   + jnp.dot(lo, bd, preferred_element_type=F32))
            qn = qq * lax.rsqrt(ssq * (1.0 / HEAD_DIM) + RMS_EPS) * qg_ref[...]
            qn = (qn * ATTN_SCALE).astype(BF16)
            outs = []
            for e in range(2):
                head = 2 * pair + e
                hmask = (lane < HEAD_DIM) if e == 0 else (lane >= HEAD_DIM)
                qh = jnp.where(hmask, qn, jnp.zeros_like(qn))
                s = lax.dot_general(qh, kn, _NT, preferred_element_type=F32)
                s = jnp.where(valid, s - slopes[head] * distf, NEG_INF)
                sink = sink_ref[head]
                m = jnp.maximum(jnp.max(s, axis=-1, keepdims=True), sink)
                p = jnp.exp(s - m)
                denom = jnp.sum(p, axis=-1, keepdims=True) + jnp.exp(sink - m)
                o = jnp.dot(p.astype(BF16), v2, preferred_element_type=F32)
                outs.append(o / denom)
            o_ref[0, :, pair * LANES:(pair + 1) * LANES] = jnp.where(
                lane < HEAD_DIM, outs[0], outs[1]).astype(o_ref.dtype)


def sw_attention(qkv, sinks, q_gain, k_gain, batch, seq):
    nb = seq // BLOCK
    qw = SW_Q_HEADS * HEAD_DIM
    kw = SW_KV_HEADS * LANES
    bd = jnp.asarray(np.kron(np.eye(2), np.ones((HEAD_DIM, HEAD_DIM))), dtype=BF16)
    prev = lambda b, n: (b, jnp.maximum(n - 1, 0), qw // kw)
    cur = lambda b, n: (b, n, qw // kw)
    prev_v = lambda b, n: (b, jnp.maximum(n - 1, 0), qw // kw + 1)
    cur_v = lambda b, n: (b, n, qw // kw + 1)
    vec = lambda g: jnp.tile(g.astype(F32), 2).reshape(1, LANES)
    return pl.pallas_call(
        _sw_kernel,
        grid=(batch, nb),
        in_specs=[pl.BlockSpec(memory_space=pltpu.SMEM),
                  pl.BlockSpec((1, BLOCK, qw), lambda b, n: (b, n, 0)),
                  pl.BlockSpec((1, BLOCK, kw), prev),
                  pl.BlockSpec((1, BLOCK, kw), cur),
                  pl.BlockSpec((1, BLOCK, kw), prev_v),
                  pl.BlockSpec((1, BLOCK, kw), cur_v),
                  pl.BlockSpec((1, LANES), lambda b, n: (0, 0)),
                  pl.BlockSpec((1, LANES), lambda b, n: (0, 0)),
                  pl.BlockSpec((LANES, LANES), lambda b, n: (0, 0))],
        out_specs=pl.BlockSpec((1, BLOCK, qw), lambda b, n: (b, n, 0)),
        out_shape=jax.ShapeDtypeStruct((batch, seq, qw), BF16),
        compiler_params=_params("parallel", "arbitrary"),
        name="sw_attention",
    )(sinks.astype(F32), qkv, qkv, qkv, qkv, qkv, vec(q_gain), vec(k_gain), bd)


def _sw_qkv_weight(w):
    d = w.shape[0]
    qw = SW_Q_HEADS * HEAD_DIM
    kvw = SW_KV_HEADS * HEAD_DIM
    dup = lambda a: jnp.broadcast_to(
        a.reshape(d, SW_KV_HEADS, 1, HEAD_DIM), (d, SW_KV_HEADS, 2, HEAD_DIM)).reshape(d, 2 * kvw)
    return jnp.concatenate([w[:, :qw], dup(w[:, qw:qw + kvw]), dup(w[:, qw + kvw:])], axis=1)


def _proj_kernel(o_ref, wo_ref, h_ref, g_ref, wq_ref, hn_ref, xn_ref, q_ref):
    hn = h_ref[...] + jnp.dot(o_ref[...], wo_ref[...], preferred_element_type=F32)
    hn_ref[...] = hn
    xn = (hn * _rms_scale(hn) * g_ref[...]).astype(BF16)
    xn_ref[...] = xn
    q_ref[...] = jnp.dot(xn, wq_ref[...], preferred_element_type=F32).astype(BF16)


def proj_residual_query(o, wo, h, g, wq, tm=512):
    t, d = h.shape
    nq = wq.shape[1]
    tm = min(tm, t)
    row = lambda i: (i, 0)
    fixed = lambda i: (0, 0)
    return pl.pallas_call(
        _proj_kernel,
        grid=(t // tm,),
        in_specs=[pl.BlockSpec((tm, d), row), pl.BlockSpec((d, d), fixed),
                  pl.BlockSpec((tm, d), row), pl.BlockSpec((1, d), fixed),
                  pl.BlockSpec((d, nq), fixed)],
        out_specs=[pl.BlockSpec((tm, d), row), pl.BlockSpec((tm, d), row),
                   pl.BlockSpec((tm, nq), row)],
        out_shape=[jax.ShapeDtypeStruct((t, d), F32), jax.ShapeDtypeStruct((t, d), BF16),
                   jax.ShapeDtypeStruct((t, nq), BF16)],
        compiler_params=_params("parallel"),
        name="proj_residual_query",
    )(o, wo, h, g.reshape(1, d), wq)


def _cand_tables():
    k = PEER_TOPK
    flat = np.zeros((80,), np.float32)
    neg = np.zeros((80,), np.float32)
    for a in range(k):
        flat[a] = a * k
    for b in range(1, 8):
        for a in range(8):
            r = 16 + 8 * (b - 1) + a
            flat[r] = a * k + b
            if (a + 1) * (b + 1) > k:
                neg[r] = NEG_INF
    for b in range(8, k):
        flat[72 + b - 8] = b
    tile = lambda v: jnp.asarray(np.repeat(v[:, None], LANES, axis=1))
    return tile(flat), tile(neg)


def _dup_words(x):
    b = lax.bitcast_convert_type(x.astype(BF16).astype(F32), jnp.uint32)
    return b | (b >> 16)


def _pair_words(x, stage):
    half = x.shape[0] // 2
    stage[...] = x.astype(BF16).astype(F32)
    ev = lax.bitcast_convert_type(stage[pl.ds(0, half, stride=2), :], jnp.uint32)
    od = lax.bitcast_convert_type(stage[pl.ds(1, half, stride=2), :], jnp.uint32)
    return (ev >> 16) | (od & jnp.uint32(0xFFFF0000))


def _topk_kernel(q_ref, sk_ref, flat_ref, neg_ref, ns_ref, e1_ref, gt_ref,
                 s_scr, tv_scr, ix_scr, cand_scr, sel_scr, stage):
    tt = LANES
    nl = 2 * PEER_HEADS
    iota = lax.broadcasted_iota(jnp.int32, (PEER_KEYS, tt), 0).astype(F32)
    big = float(4 * PEER_KEYS * PEER_KEYS)

    for l in range(nl):
        s_scr[l] = lax.dot_general(sk_ref[l], q_ref[:, l * PEER_HALF:(l + 1) * PEER_HALF],
                                   _NT, preferred_element_type=F32)

    def l1_body(r, c):
        for l in range(nl):
            s = s_scr[l]
            m = jnp.max(s, axis=0, keepdims=True)
            ix = jnp.min(jnp.where(s == m, iota, big), axis=0, keepdims=True)
            s_scr[l] = jnp.where(iota == ix, NEG_INF, s)
            tv_scr[l, pl.ds(r, 1), :] = m
            ix_scr[l, pl.ds(r, 1), :] = ix
        return c

    lax.fori_loop(0, PEER_TOPK, l1_body, 0)

    flat = flat_ref[...]
    for h in range(PEER_HEADS):
        tv1 = tv_scr[2 * h]
        tv2 = tv_scr[2 * h + 1]
        parts = [tv1 + tv2[0:1, :]]
        for b in range(1, 8):
            parts.append(tv1[0:8, :] + tv2[b:b + 1, :])
        parts.append(tv1[0:1, :] + tv2[8:16, :])
        cand_scr[h] = jnp.concatenate(parts, axis=0) + neg_ref[...]
        sel_scr[h] = jnp.zeros((80, tt), F32)

    def l2_body(r, c):
        for h in range(PEER_HEADS):
            cd = cand_scr[h]
            m = jnp.max(cd, axis=0, keepdims=True)
            ix = jnp.min(jnp.where(cd == m, flat, big), axis=0, keepdims=True)
            hit = flat == ix
            cand_scr[h] = jnp.where(hit, NEG_INF, cd)
            sel_scr[h] = jnp.where(hit, 1.0, sel_scr[h])
        return c

    lax.fori_loop(0, PEER_TOPK, l2_body, 0)

    row8 = lax.broadcasted_iota(jnp.int32, (8, tt), 0)
    for h in range(PEER_HEADS):
        tv1 = tv_scr[2 * h]
        tv2 = tv_scr[2 * h + 1]
        ix1 = ix_scr[2 * h]
        ix2 = ix_scr[2 * h + 1]
        sel = sel_scr[h]
        x1 = jnp.exp(tv1 - tv1[0:1, :])
        x2 = jnp.exp(tv2 - tv2[0:1, :])
        eparts = [x1 * x2[0:1, :]]
        for b in range(1, 8):
            eparts.append(x1[0:8, :] * x2[b:b + 1, :])
        eparts.append(x1[0:1, :] * x2[8:16, :])
        z = jnp.sum(sel * jnp.concatenate(eparts, axis=0), axis=0, keepdims=True)
        n_lo = sel[0:8, :]
        for b in range(1, 8):
            n_lo = n_lo + sel[16 + 8 * (b - 1):24 + 8 * (b - 1), :]
        n_lo = n_lo + jnp.where(row8 == 0, jnp.sum(sel[72:80, :], axis=0, keepdims=True), 0.0)
        n_a = jnp.concatenate([n_lo, sel[8:16, :]], axis=0)
        s1 = lax.dot_general(sk_ref[2 * h], q_ref[:, (2 * h) * PEER_HALF:(2 * h + 1) * PEER_HALF],
                             _NT, preferred_element_type=F32)
        s2 = lax.dot_general(sk_ref[2 * h + 1],
                             q_ref[:, (2 * h + 1) * PEER_HALF:(2 * h + 2) * PEER_HALF],
                             _NT, preferred_element_type=F32)
        ns = jnp.zeros((PEER_KEYS, tt), F32)
        r2 = jnp.full((PEER_KEYS, tt), float(PEER_TOPK), F32)
        for a in range(PEER_TOPK):
            ns = jnp.where(iota == ix1[a:a + 1, :], n_a[a:a + 1, :], ns)
            r2 = jnp.where(iota == ix2[a:a + 1, :], float(a), r2)
        ns_ref[h] = _dup_words(ns)
        e1_ref[h] = _dup_words(jnp.exp(s1 - tv1[0:1, :]) / z)
        for w, tab in enumerate((r2, jnp.exp(s2 - tv2[0:1, :]))):
            words = _pair_words(tab, stage)
            for k in range(PEER_KEYS // 16):
                gt_ref[_gt_row(k, h, w):_gt_row(k, h, w) + 8, :] = words[8 * k:8 * k + 8, :]


GT_ROWS = (PEER_KEYS // 16) * PEER_HEADS * 2 * 8


def _gt_row(k, h, w):
    return ((k * PEER_HEADS + h) * 2 + w) * 8


def peer_topk(q, sub_keys):
    t = q.shape[0]
    tt = LANES
    nl = 2 * PEER_HEADS
    flat, neg = _cand_tables()
    out_w = jax.ShapeDtypeStruct((PEER_HEADS, PEER_KEYS, t), jnp.uint32)
    out_g = jax.ShapeDtypeStruct((t // tt * GT_ROWS, tt), jnp.uint32)
    ospec = pl.BlockSpec((PEER_HEADS, PEER_KEYS, tt), lambda i: (0, 0, i))
    return pl.pallas_call(
        _topk_kernel,
        grid=(t // tt,),
        in_specs=[pl.BlockSpec((tt, nl * PEER_HALF), lambda i: (i, 0)),
                  pl.BlockSpec((nl, PEER_KEYS, PEER_HALF), lambda i: (0, 0, 0)),
                  pl.BlockSpec((80, tt), lambda i: (0, 0)),
                  pl.BlockSpec((80, tt), lambda i: (0, 0))],
        out_specs=[ospec, ospec, pl.BlockSpec((GT_ROWS, tt), lambda i: (i, 0))],
        out_shape=[out_w, out_w, out_g],
        scratch_shapes=[pltpu.VMEM((nl, PEER_KEYS, tt), F32),
                        pltpu.VMEM((nl, PEER_TOPK, tt), F32),
                        pltpu.VMEM((nl, PEER_TOPK, tt), F32),
                        pltpu.VMEM((PEER_HEADS, 80, tt), F32),
                        pltpu.VMEM((PEER_HEADS, 80, tt), F32),
                        pltpu.VMEM((PEER_KEYS, tt), F32)],
        compiler_params=_params("parallel"),
        name="peer_topk",
    )(q, sub_keys, flat, neg)


GELU_C = float(np.sqrt(2.0 / np.pi))


def _peer_kernel(xn_ref, u_ref, vt_ref, ns_ref, e1_ref, gt_ref, h_ref, o_ref,
                 acc_ref, coef_ref, *, et, tt):
    j = pl.program_id(1)
    rows = et // PEER_KEYS

    @pl.when(j == 0)
    def _():
        acc_ref[...] = jnp.zeros_like(acc_ref)

    base = pl.multiple_of(j * rows, rows)
    pk = 16
    zero = jnp.zeros((pk, LANES), BF16)

    def bcast(words, r):
        return pltpu.bitcast(jnp.broadcast_to(words[r:r + 1, :], (pk // 2, LANES)), BF16)

    ht = lax.dot_general(u_ref[...], xn_ref[...], _NT, preferred_element_type=F32)
    for c in range(tt // LANES):
        cs = slice(c * LANES, (c + 1) * LANES)
        ns_rows = [ns_ref[h, pl.ds(base, rows), cs] for h in range(PEER_HEADS)]
        e1_rows = [e1_ref[h, pl.ds(base, rows), cs] for h in range(PEER_HEADS)]
        for r in range(rows):
            ns = [bcast(ns_rows[h], r) for h in range(PEER_HEADS)]
            e1 = [bcast(e1_rows[h], r) for h in range(PEER_HEADS)]
            for k in range(PEER_KEYS // pk):
                g = zero
                for h in range(PEER_HEADS):
                    r2 = pltpu.bitcast(gt_ref[pl.ds(c * GT_ROWS + _gt_row(k, h, 0), 8), :], BF16)
                    e2 = pltpu.bitcast(gt_ref[pl.ds(c * GT_ROWS + _gt_row(k, h, 1), 8), :], BF16)
                    g = g + jnp.where(r2 < ns[h], e2, zero) * e1[h]
                es = slice(r * PEER_KEYS + k * pk, r * PEER_KEYS + (k + 1) * pk)
                x = ht[es, cs]
                t = jnp.tanh(x * (GELU_C + (GELU_C * 0.044715) * (x * x))).astype(BF16)
                hx = x.astype(BF16) * 0.5
                coef_ref[es, cs] = g * (hx + hx * t)
    acc_ref[...] += jnp.dot(vt_ref[...], coef_ref[...], preferred_element_type=F32)

    @pl.when(j == pl.num_programs(1) - 1)
    def _():
        o_ref[...] = h_ref[...] + acc_ref[...].T


def peer_mix(xn, u, vt, ns, e1, gt, h, et=1024, tt=512):
    t, d = h.shape
    ne = u.shape[0]
    tt = min(tt, t)
    assert et % (8 * PEER_KEYS) == 0 and ne % et == 0 and t % tt == 0
    gspec = pl.BlockSpec((PEER_HEADS, PEER_KEYS, tt), lambda i, j: (0, 0, i))
    return pl.pallas_call(
        functools.partial(_peer_kernel, et=et, tt=tt),
        grid=(t // tt, ne // et),
        in_specs=[pl.BlockSpec((tt, d), lambda i, j: (i, 0)),
                  pl.BlockSpec((et, d), lambda i, j: (j, 0)),
                  pl.BlockSpec((d, et), lambda i, j: (0, j)),
                  gspec, gspec,
                  pl.BlockSpec((tt // LANES * GT_ROWS, LANES), lambda i, j: (i, 0)),
                  pl.BlockSpec((tt, d), lambda i, j: (i, 0))],
        out_specs=pl.BlockSpec((tt, d), lambda i, j: (i, 0)),
        out_shape=jax.ShapeDtypeStruct((t, d), F32),
        scratch_shapes=[pltpu.VMEM((d, tt), F32), pltpu.VMEM((et, tt), BF16)],
        compiler_params=_params("parallel", "arbitrary"),
        name="peer_mix",
    )(xn, u, vt, ns, e1, gt, h)


def _peer_block(o, wo, h, g, wq, sub_keys, u, v):
    hn, xn, q = proj_residual_query(o, wo.astype(BF16), h, g, wq.astype(BF16))
    sk = sub_keys.reshape(2 * PEER_HEADS, PEER_KEYS, PEER_HALF).astype(BF16)
    ns, e1, gt = peer_topk(q, sk)
    return peer_mix(xn, u.astype(BF16), v.T.astype(BF16), ns, e1, gt, hn)


def kernel(x, sb_norm, sb_w_qkv, sb_w_o, sw_norm, sw_w_qkv, sw_q_gain, sw_k_gain, sw_sinks, sw_w_o,
           ffn_norm, peer_w_query, peer_sub_keys, peer_u, peer_v):
    batch, seq, d = x.shape
    t = batch * seq
    h = x.reshape(t, d)

    qkv = norm_matmul(h, sb_norm[0], sb_w_qkv[0].astype(BF16), BF16)
    o = sb_attention(qkv.reshape(batch, seq, -1), batch, seq).reshape(t, -1)
    h = _peer_block(o, sb_w_o[0], h, ffn_norm[0], peer_w_query[0], peer_sub_keys[0],
                    peer_u[0], peer_v[0])

    qkv = norm_matmul(h, sw_norm[0], _sw_qkv_weight(sw_w_qkv[0]).astype(BF16), F32)
    o = sw_attention(qkv.reshape(batch, seq, -1), sw_sinks[0], sw_q_gain[0], sw_k_gain[0],
                     batch, seq).reshape(t, -1)
    h = _peer_block(o, sw_w_o[0], h, ffn_norm[1], peer_w_query[1], peer_sub_keys[1],
                    peer_u[1], peer_v[1])
    return h.reshape(batch, seq, d)
```

```python
import functools

import numpy as np
import jax
import jax.numpy as jnp
from jax import lax
from jax.experimental import pallas as pl
from jax.experimental.pallas import tpu as pltpu

F32 = jnp.float32
BF16 = jnp.bfloat16

D_MODEL = 1024
HEAD_DIM = 64
SB_HEADS = 16
SW_Q_HEADS = 16
SW_KV_HEADS = 4
BLOCK = 128
LANES = 128
PEER_HEADS = 8
PEER_KEYS = 128
PEER_HALF = 128
PEER_TOPK = 16
RMS_EPS = 1e-6
ATTN_SCALE = HEAD_DIM ** -0.5
SB_DEAD_LOG = -110.0
NEG_INF = float("-inf")
VMEM_LIMIT = 48 * 1024 * 1024

_NT = (((1,), (1,)), ((), ()))


def _params(*sem, flags=None):
    return pltpu.CompilerParams(dimension_semantics=sem, vmem_limit_bytes=VMEM_LIMIT, flags=flags)


def _rms_scale(x):
    return lax.rsqrt(jnp.mean(x * x, axis=-1, keepdims=True) + RMS_EPS)


def _norm_matmul_kernel(x_ref, g_ref, w_ref, o_ref):
    x = x_ref[...]
    xn = (x * _rms_scale(x) * g_ref[...]).astype(BF16)
    o_ref[...] = jnp.dot(xn, w_ref[...], preferred_element_type=F32).astype(o_ref.dtype)


def norm_matmul(x, g, w, out_dtype, tm=512):
    t, d = x.shape
    n = w.shape[1]
    tm = min(tm, t)
    return pl.pallas_call(
        _norm_matmul_kernel,
        grid=(t // tm,),
        in_specs=[pl.BlockSpec((tm, d), lambda i: (i, 0)),
                  pl.BlockSpec((1, d), lambda i: (0, 0)),
                  pl.BlockSpec((d, n), lambda i: (0, 0))],
        out_specs=pl.BlockSpec((tm, n), lambda i: (i, 0)),
        out_shape=jax.ShapeDtypeStruct((t, n), out_dtype),
        compiler_params=_params("parallel"),
        name="norm_matmul",
    )(x, g.reshape(1, d), w)


def _sb_kernel(q_ref, k_ref, v_ref, tri1_ref, tri2_ref, o_ref):
    i = pl.program_id(2)
    tq = BLOCK
    lane = lax.broadcasted_iota(jnp.int32, (tq, LANES), 1)
    iota = lambda w, d: lax.broadcasted_iota(jnp.int32, (tq, w), d)
    causal1 = iota(tq, 1) < iota(tq, 0)
    causal2 = iota(2 * tq, 1) < iota(2 * tq, 0) + tq

    def span(qh, start, w, carry, acc, mask):
        kj = k_ref[0, pl.ds(start, w), :]
        vj = v_ref[0, pl.ds(start, w), :]
        z = lax.dot_general(qh, kj, _NT, preferred_element_type=F32)
        lf = -(jnp.maximum(z, 0.0) + jnp.log1p(jnp.exp(-jnp.abs(z))))
        lfm = lf if mask is None else jnp.where(mask, lf, 0.0)
        hi = lfm.astype(BF16)
        lo = (lfm - hi.astype(F32)).astype(BF16)
        tri = tri1_ref[...] if w == tq else tri2_ref[...]
        wt = jnp.dot(jnp.concatenate([hi, lo], axis=1), tri, preferred_element_type=F32)
        cw = carry if w == tq else jnp.concatenate([carry, carry], axis=1)
        a = jnp.exp(z + lf + wt[:, :w] + cw)
        if mask is not None:
            a = jnp.where(mask, a, 0.0)
        acc = acc + jnp.dot(a.astype(BF16), vj, preferred_element_type=F32)
        return carry + wt[:, w:], acc

    qs = []
    for u in range(2):
        q2 = (q_ref[0, u * tq:(u + 1) * tq, :].astype(F32) * ATTN_SCALE).astype(BF16)
        for hh in range(2):
            hmask = (lane < HEAD_DIM) if hh == 0 else (lane >= HEAD_DIM)
            qs.append(jnp.where(hmask, q2, jnp.zeros_like(q2)))
    zero = jnp.zeros((tq, LANES), F32)
    base = pl.multiple_of(i * (2 * tq), 2 * tq)
    carries, accs = [], []
    for c in range(4):
        if c < 2:
            carry, acc = span(qs[c], base, tq, zero, zero, causal1)
        else:
            carry, acc = span(qs[c], base, 2 * tq, zero, zero, causal2)
        carries.append(carry)
        accs.append(acc)

    def alive(cs):
        return jnp.max(jnp.maximum(jnp.maximum(cs[0], cs[1]), jnp.maximum(cs[2], cs[3])))

    def cond(st):
        r, _, _, m = st
        return jnp.logical_and(r < i, m > SB_DEAD_LOG)

    def body(st):
        r, cs, ac, _ = st
        cs, ac = list(cs), list(ac)
        start = pl.multiple_of((i - 1 - r) * (2 * tq), 2 * tq)
        for c in range(4):
            cs[c], ac[c] = span(qs[c], start, 2 * tq, cs[c], ac[c], None)
        return r + 1, tuple(cs), tuple(ac), alive(cs)

    _, _, accs, _ = lax.while_loop(
        cond, body, (jnp.int32(0), tuple(carries), tuple(accs), alive(carries)))
    for u in range(2):
        o_ref[0, u * tq:(u + 1) * tq, :] = jnp.where(
            lane < HEAD_DIM, accs[2 * u], accs[2 * u + 1]).astype(o_ref.dtype)


def _sb_tri(w):
    r = np.arange(2 * w)[:, None] % w
    c = np.arange(w + LANES)[None, :]
    return jnp.asarray(np.where(c < w, r > c, True), dtype=BF16)


def sb_attention(qkv, batch, seq):
    npair = SB_HEADS * HEAD_DIM // LANES
    return pl.pallas_call(
        _sb_kernel,
        grid=(batch, npair, seq // (2 * BLOCK)),
        in_specs=[pl.BlockSpec((1, 2 * BLOCK, LANES), lambda b, p, i: (b, i, p)),
                  pl.BlockSpec((1, seq, LANES), lambda b, p, i: (b, 0, npair + p)),
                  pl.BlockSpec((1, seq, LANES), lambda b, p, i: (b, 0, 2 * npair + p)),
                  pl.BlockSpec((2 * BLOCK, 2 * BLOCK), lambda b, p, i: (0, 0)),
                  pl.BlockSpec((4 * BLOCK, 3 * BLOCK), lambda b, p, i: (0, 0))],
        out_specs=pl.BlockSpec((1, 2 * BLOCK, LANES), lambda b, p, i: (b, i, p)),
        out_shape=jax.ShapeDtypeStruct((batch, seq, SB_HEADS * HEAD_DIM), BF16),
        compiler_params=_params("parallel", "parallel", "arbitrary"),
        name="sb_attention",
    )(qkv, qkv, qkv, _sb_tri(BLOCK), _sb_tri(2 * BLOCK))


def _alibi_slopes(n_heads):
    return [float(v) for v in np.asarray(
        2.0 ** (-8.0 * np.arange(1, n_heads + 1) / n_heads), dtype=np.float32)]


def _sw_kernel(sink_ref, q_ref, kp_ref, kc_ref, vp_ref, vc_ref, qg_ref, kg_ref, bd_ref, o_ref):
    n = pl.program_id(1)
    lane = lax.broadcasted_iota(jnp.int32, (BLOCK, LANES), 1)
    ii = lax.broadcasted_iota(jnp.int32, (BLOCK, 2 * BLOCK), 0)
    jj = lax.broadcasted_iota(jnp.int32, (BLOCK, 2 * BLOCK), 1)
    dist = BLOCK + ii - jj
    valid = (dist >= 0) & (dist < BLOCK) & jnp.logical_or(jj >= BLOCK, n > 0)
    distf = dist.astype(F32)
    slopes = _alibi_slopes(SW_Q_HEADS)
    bd = bd_ref[...]
    group = SW_Q_HEADS // SW_KV_HEADS
    for kk in range(SW_KV_HEADS):
        cs = slice(kk * LANES, (kk + 1) * LANES)
        k2 = jnp.concatenate([kp_ref[0, :, cs], kc_ref[0, :, cs]], axis=0)
        kn = (k2 * _rms_scale(k2) * kg_ref[...]).astype(BF16)
        v2 = jnp.concatenate([vp_ref[0, :, cs], vc_ref[0, :, cs]], axis=0).astype(BF16)
        for gp in range(group // 2):
            pair = kk * (group // 2) + gp
            qq = q_ref[0, :, pair * LANES:(pair + 1) * LANES]
            sq = qq * qq
            hi = sq.astype(BF16)
            lo = (sq - hi.astype(F32)).astype(BF16)
            ssq = (jnp.dot(hi, bd, preferred_element_type=F32)
                   + jnp.dot(lo, bd, preferred_element_type=F32))
            qn = qq * lax.rsqrt(ssq * (1.0 / HEAD_DIM) + RMS_EPS) * qg_ref[...]
            qn = (qn * ATTN_SCALE).astype(BF16)
            outs = []
            for e in range(2):
                head = 2 * pair + e
                hmask = (lane < HEAD_DIM) if e == 0 else (lane >= HEAD_DIM)
                qh = jnp.where(hmask, qn, jnp.zeros_like(qn))
                s = lax.dot_general(qh, kn, _NT, preferred_element_type=F32)
                s = jnp.where(valid, s - slopes[head] * distf, NEG_INF)
                sink = sink_ref[head]
                m = jnp.maximum(jnp.max(s, axis=-1, keepdims=True), sink)
                p = jnp.exp(s - m)
                denom = jnp.sum(p, axis=-1, keepdims=True) + jnp.exp(sink - m)
                o = jnp.dot(p.astype(BF16), v2, preferred_element_type=F32)
                outs.append(o / denom)
            o_ref[0, :, pair * LANES:(pair + 1) * LANES] = jnp.where(
                lane < HEAD_DIM, outs[0], outs[1]).astype(o_ref.dtype)


def sw_attention(qkv, sinks, q_gain, k_gain, batch, seq):
    nb = seq // BLOCK
    qw = SW_Q_HEADS * HEAD_DIM
    kw = SW_KV_HEADS * LANES
    bd = jnp.asarray(np.kron(np.eye(2), np.ones((HEAD_DIM, HEAD_DIM))), dtype=BF16)
    prev = lambda b, n: (b, jnp.maximum(n - 1, 0), qw // kw)
    cur = lambda b, n: (b, n, qw // kw)
    prev_v = lambda b, n: (b, jnp.maximum(n - 1, 0), qw // kw + 1)
    cur_v = lambda b, n: (b, n, qw // kw + 1)
    vec = lambda g: jnp.tile(g.astype(F32), 2).reshape(1, LANES)
    return pl.pallas_call(
        _sw_kernel,
        grid=(batch, nb),
        in_specs=[pl.BlockSpec(memory_space=pltpu.SMEM),
                  pl.BlockSpec((1, BLOCK, qw), lambda b, n: (b, n, 0)),
                  pl.BlockSpec((1, BLOCK, kw), prev),
                  pl.BlockSpec((1, BLOCK, kw), cur),
                  pl.BlockSpec((1, BLOCK, kw), prev_v),
                  pl.BlockSpec((1, BLOCK, kw), cur_v),
                  pl.BlockSpec((1, LANES), lambda b, n: (0, 0)),
                  pl.BlockSpec((1, LANES), lambda b, n: (0, 0)),
                  pl.BlockSpec((LANES, LANES), lambda b, n: (0, 0))],
        out_specs=pl.BlockSpec((1, BLOCK, qw), lambda b, n: (b, n, 0)),
        out_shape=jax.ShapeDtypeStruct((batch, seq, qw), BF16),
        compiler_params=_params("parallel", "arbitrary"),
        name="sw_attention",
    )(sinks.astype(F32), qkv, qkv, qkv, qkv, qkv, vec(q_gain), vec(k_gain), bd)


def _sw_qkv_weight(w):
    d = w.shape[0]
    qw = SW_Q_HEADS * HEAD_DIM
    kvw = SW_KV_HEADS * HEAD_DIM
    dup = lambda a: jnp.broadcast_to(
        a.reshape(d, SW_KV_HEADS, 1, HEAD_DIM), (d, SW_KV_HEADS, 2, HEAD_DIM)).reshape(d, 2 * kvw)
    return jnp.concatenate([w[:, :qw], dup(w[:, qw:qw + kvw]), dup(w[:, qw + kvw:])], axis=1)


def _proj_kernel(o_ref, wo_ref, h_ref, g_ref, wq_ref, hn_ref, xn_ref, q_ref):
    hn = h_ref[...] + jnp.dot(o_ref[...], wo_ref[...], preferred_element_type=F32)
    hn_ref[...] = hn
    xn = (hn * _rms_scale(hn) * g_ref[...]).astype(BF16)
    xn_ref[...] = xn
    q_ref[...] = jnp.dot(xn, wq_ref[...], preferred_element_type=F32).astype(BF16)


def proj_residual_query(o, wo, h, g, wq, tm=512):
    t, d = h.shape
    nq = wq.shape[1]
    tm = min(tm, t)
    row = lambda i: (i, 0)
    fixed = lambda i: (0, 0)
    return pl.pallas_call(
        _proj_kernel,
        grid=(t // tm,),
        in_specs=[pl.BlockSpec((tm, d), row), pl.BlockSpec((d, d), fixed),
                  pl.BlockSpec((tm, d), row), pl.BlockSpec((1, d), fixed),
                  pl.BlockSpec((d, nq), fixed)],
        out_specs=[pl.BlockSpec((tm, d), row), pl.BlockSpec((tm, d), row),
                   pl.BlockSpec((tm, nq), row)],
        out_shape=[jax.ShapeDtypeStruct((t, d), F32), jax.ShapeDtypeStruct((t, d), BF16),
                   jax.ShapeDtypeStruct((t, nq), BF16)],
        compiler_params=_params("parallel"),
        name="proj_residual_query",
    )(o, wo, h, g.reshape(1, d), wq)


def _cand_tables():
    k = PEER_TOPK
    flat = np.zeros((80,), np.float32)
    neg = np.zeros((80,), np.float32)
    for a in range(k):
        flat[a] = a * k
    for b in range(1, 8):
        for a in range(8):
            r = 16 + 8 * (b - 1) + a
            flat[r] = a * k + b
            if (a + 1) * (b + 1) > k:
                neg[r] = NEG_INF
    for b in range(8, k):
        flat[72 + b - 8] = b
    tile = lambda v: jnp.asarray(np.repeat(v[:, None], LANES, axis=1))
    return tile(flat), tile(neg)


def _dup_words(x):
    b = lax.bitcast_convert_type(x.astype(BF16).astype(F32), jnp.uint32)
    return b | (b >> 16)


def _pair_words(x, stage):
    half = x.shape[0] // 2
    stage[...] = x.astype(BF16).astype(F32)
    ev = lax.bitcast_convert_type(stage[pl.ds(0, half, stride=2), :], jnp.uint32)
    od = lax.bitcast_convert_type(stage[pl.ds(1, half, stride=2), :], jnp.uint32)
    return (ev >> 16) | (od & jnp.uint32(0xFFFF0000))


def _topk_kernel(q_ref, sk_ref, flat_ref, neg_ref, ns_ref, e1_ref, gt_ref,
                 s_scr, rk_scr, tv_scr, cand_scr, sel_scr, stage):
    tt = LANES
    nl = 2 * PEER_HEADS
    iota = lax.broadcasted_iota(jnp.int32, (PEER_KEYS, tt), 0).astype(F32)
    big = float(4 * PEER_KEYS * PEER_KEYS)
    unranked = float(PEER_TOPK)

    def scores(l):
        return lax.dot_general(sk_ref[l], q_ref[:, l * PEER_HALF:(l + 1) * PEER_HALF],
                               _NT, preferred_element_type=F32)

    def l1_reset(l):
        s_scr[l] = scores(l)
        rk_scr[l] = jnp.full((PEER_KEYS, tt), unranked, F32)

    def l1_round(l, r, exact):
        s = s_scr[l]
        m = jnp.max(s, axis=0, keepdims=True)
        hit = s == m
        if exact:
            hit = iota == jnp.min(jnp.where(hit, iota, big), axis=0, keepdims=True)
        s_scr[l] = jnp.where(hit, NEG_INF, s)
        rk_scr[l] = jnp.where(hit, r.astype(F32), rk_scr[l])
        tv_scr[l, pl.ds(r, 1), :] = m

    def miscount(x):
        return jnp.abs(jnp.sum(x, axis=0, keepdims=True) - float(PEER_TOPK))

    def redo_where_miscounted(counts, redo):
        worst = counts[0]
        for c in counts[1:]:
            worst = jnp.maximum(worst, c)

        @pl.when(jnp.max(worst) > 0.0)
        def _():
            for idx, c in enumerate(counts):
                pl.when(jnp.max(c) > 0.0)(functools.partial(redo, idx))

    for l in range(nl):
        l1_reset(l)

    def l1_fast(r, c):
        for l in range(nl):
            l1_round(l, r, False)
        return c

    lax.fori_loop(0, PEER_TOPK, l1_fast, 0)

    def l1_redo(l):
        l1_reset(l)

        def l1_exact(r, c):
            l1_round(l, r, True)
            return c

        lax.fori_loop(0, PEER_TOPK, l1_exact, 0)

    redo_where_miscounted(
        [miscount(jnp.where(rk_scr[l] < unranked, 1.0, 0.0)) for l in range(nl)], l1_redo)

    flat = flat_ref[...]

    def l2_reset(h):
        tv1 = tv_scr[2 * h]
        tv2 = tv_scr[2 * h + 1]
        parts = [tv1 + tv2[0:1, :]]
        for b in range(1, 8):
            parts.append(tv1[0:8, :] + tv2[b:b + 1, :])
        parts.append(tv1[0:1, :] + tv2[8:16, :])
        cand_scr[h] = jnp.concatenate(parts, axis=0) + neg_ref[...]
        sel_scr[h] = jnp.zeros((80, tt), F32)

    def l2_round(h, exact):
        cd = cand_scr[h]
        m = jnp.max(cd, axis=0, keepdims=True)
        hit = cd == m
        if exact:
            hit = flat == jnp.min(jnp.where(hit, flat, big), axis=0, keepdims=True)
        cand_scr[h] = jnp.where(hit, NEG_INF, cd)
        sel_scr[h] = jnp.where(hit, 1.0, sel_scr[h])

    for h in range(PEER_HEADS):
        l2_reset(h)

    def l2_fast(r, c):
        for h in range(PEER_HEADS):
            l2_round(h, False)
        return c

    lax.fori_loop(0, PEER_TOPK, l2_fast, 0)

    def l2_redo(h):
        l2_reset(h)

        def l2_exact(r, c):
            l2_round(h, True)
            return c

        lax.fori_loop(0, PEER_TOPK, l2_exact, 0)

    redo_where_miscounted([miscount(sel_scr[h]) for h in range(PEER_HEADS)], l2_redo)

    row8 = lax.broadcasted_iota(jnp.int32, (8, tt), 0)
    for h in range(PEER_HEADS):
        tv1 = tv_scr[2 * h]
        tv2 = tv_scr[2 * h + 1]
        sel = sel_scr[h]
        x1 = jnp.exp(tv1 - tv1[0:1, :])
        x2 = jnp.exp(tv2 - tv2[0:1, :])
        eparts = [x1 * x2[0:1, :]]
        for b in range(1, 8):
            eparts.append(x1[0:8, :] * x2[b:b + 1, :])
        eparts.append(x1[0:1, :] * x2[8:16, :])
        z = jnp.sum(sel * jnp.concatenate(eparts, axis=0), axis=0, keepdims=True)
        n_lo = sel[0:8, :]
        for b in range(1, 8):
            n_lo = n_lo + sel[16 + 8 * (b - 1):24 + 8 * (b - 1), :]
        n_lo = n_lo + jnp.where(row8 == 0, jnp.sum(sel[72:80, :], axis=0, keepdims=True), 0.0)
        s1 = lax.dot_general(sk_ref[2 * h], q_ref[:, (2 * h) * PEER_HALF:(2 * h + 1) * PEER_HALF],
                             _NT, preferred_element_type=F32)
        s2 = lax.dot_general(sk_ref[2 * h + 1],
                             q_ref[:, (2 * h + 1) * PEER_HALF:(2 * h + 2) * PEER_HALF],
                             _NT, preferred_element_type=F32)
        rank1 = rk_scr[2 * h]
        r2 = rk_scr[2 * h + 1]
        a0 = jnp.sum(sel[0:PEER_TOPK, :], axis=0, keepdims=True)
        ns = jnp.where(rank1 < a0, 1.0, 0.0)
        for a in range(8):
            ns = jnp.where(rank1 == float(a), n_lo[a:a + 1, :], ns)
        ns_ref[h] = _dup_words(ns)
        e1_ref[h] = _dup_words(jnp.exp(s1 - tv1[0:1, :]) / z)
        for w, tab in enumerate((r2, jnp.exp(s2 - tv2[0:1, :]))):
            words = _pair_words(tab, stage)
            for k in range(PEER_KEYS // 16):
                gt_ref[_gt_row(k, h, w):_gt_row(k, h, w) + 8, :] = words[8 * k:8 * k + 8, :]


GT_ROWS = (PEER_KEYS // 16) * PEER_HEADS * 2 * 8


def _gt_row(k, h, w):
    return ((k * PEER_HEADS + h) * 2 + w) * 8


def peer_topk(q, sub_keys):
    t = q.shape[0]
    tt = LANES
    nl = 2 * PEER_HEADS
    flat, neg = _cand_tables()
    out_w = jax.ShapeDtypeStruct((PEER_HEADS, PEER_KEYS, t), jnp.uint32)
    out_g = jax.ShapeDtypeStruct((t // tt * GT_ROWS, tt), jnp.uint32)
    ospec = pl.BlockSpec((PEER_HEADS, PEER_KEYS, tt), lambda i: (0, 0, i))
    return pl.pallas_call(
        _topk_kernel,
        grid=(t // tt,),
        in_specs=[pl.BlockSpec((tt, nl * PEER_HALF), lambda i: (i, 0)),
                  pl.BlockSpec((nl, PEER_KEYS, PEER_HALF), lambda i: (0, 0, 0)),
                  pl.BlockSpec((80, tt), lambda i: (0, 0)),
                  pl.BlockSpec((80, tt), lambda i: (0, 0))],
        out_specs=[ospec, ospec, pl.BlockSpec((GT_ROWS, tt), lambda i: (i, 0))],
        out_shape=[out_w, out_w, out_g],
        scratch_shapes=[pltpu.VMEM((nl, PEER_KEYS, tt), F32),
                        pltpu.VMEM((nl, PEER_KEYS, tt), F32),
                        pltpu.VMEM((nl, PEER_TOPK, tt), F32),
                        pltpu.VMEM((PEER_HEADS, 80, tt), F32),
                        pltpu.VMEM((PEER_HEADS, 80, tt), F32),
                        pltpu.VMEM((PEER_KEYS, tt), F32)],
        compiler_params=_params("parallel"),
        name="peer_topk",
    )(q, sub_keys, flat, neg)


GELU_C = float(np.sqrt(2.0 / np.pi))


def _peer_kernel(xn_ref, u_ref, vt_ref, ns_ref, e1_ref, gt_ref, h_ref, o_ref,
                 acc_ref, coef_ref, *, et, tt):
    j = pl.program_id(1)
    rows = et // PEER_KEYS

    @pl.when(j == 0)
    def _():
        acc_ref[...] = jnp.zeros_like(acc_ref)

    base = pl.multiple_of(j * rows, rows)
    pk = 16
    zero = jnp.zeros((pk, LANES), BF16)

    def bcast(words, r):
        return pltpu.bitcast(jnp.broadcast_to(words[r:r + 1, :], (pk // 2, LANES)), BF16)

    ht = lax.dot_general(u_ref[...], xn_ref[...], _NT, preferred_element_type=F32)
    for c in range(tt // LANES):
        cs = slice(c * LANES, (c + 1) * LANES)
        ns_rows = [ns_ref[h, pl.ds(base, rows), cs] for h in range(PEER_HEADS)]
        e1_rows = [e1_ref[h, pl.ds(base, rows), cs] for h in range(PEER_HEADS)]
        for r in range(rows):
            ns = [bcast(ns_rows[h], r) for h in range(PEER_HEADS)]
            e1 = [bcast(e1_rows[h], r) for h in range(PEER_HEADS)]
            for k in range(PEER_KEYS // pk):
                g = zero
                for h in range(PEER_HEADS):
                    r2 = pltpu.bitcast(gt_ref[pl.ds(c * GT_ROWS + _gt_row(k, h, 0), 8), :], BF16)
                    e2 = pltpu.bitcast(gt_ref[pl.ds(c * GT_ROWS + _gt_row(k, h, 1), 8), :], BF16)
                    g = g + jnp.where(r2 < ns[h], e2, zero) * e1[h]
                es = slice(r * PEER_KEYS + k * pk, r * PEER_KEYS + (k + 1) * pk)
                x = ht[es, cs]
                t = jnp.tanh(x * (GELU_C + (GELU_C * 0.044715) * (x * x))).astype(BF16)
                hx = x.astype(BF16) * 0.5
                coef_ref[es, cs] = g * (hx + hx * t)
    acc_ref[...] += jnp.dot(vt_ref[...], coef_ref[...], preferred_element_type=F32)

    @pl.when(j == pl.num_programs(1) - 1)
    def _():
        o_ref[...] = h_ref[...] + acc_ref[...].T


def peer_mix(xn, u, vt, ns, e1, gt, h, et=1024, tt=512):
    t, d = h.shape
    ne = u.shape[0]
    tt = min(tt, t)
    assert et % (8 * PEER_KEYS) == 0 and ne % et == 0 and t % tt == 0
    gspec = pl.BlockSpec((PEER_HEADS, PEER_KEYS, tt), lambda i, j: (0, 0, i))
    return pl.pallas_call(
        functools.partial(_peer_kernel, et=et, tt=tt),
        grid=(t // tt, ne // et),
        in_specs=[pl.BlockSpec((tt, d), lambda i, j: (i, 0)),
                  pl.BlockSpec((et, d), lambda i, j: (j, 0)),
                  pl.BlockSpec((d, et), lambda i, j: (0, j)),
                  gspec, gspec,
                  pl.BlockSpec((tt // LANES * GT_ROWS, LANES), lambda i, j: (i, 0)),
                  pl.BlockSpec((tt, d), lambda i, j: (i, 0))],
        out_specs=pl.BlockSpec((tt, d), lambda i, j: (i, 0)),
        out_shape=jax.ShapeDtypeStruct((t, d), F32),
        scratch_shapes=[pltpu.VMEM((d, tt), F32), pltpu.VMEM((et, tt), BF16)],
        compiler_params=_params("parallel", "arbitrary"),
        name="peer_mix",
    )(xn, u, vt, ns, e1, gt, h)


def _peer_block(o, wo, h, g, wq, sub_keys, u, v):
    hn, xn, q = proj_residual_query(o, wo.astype(BF16), h, g, wq.astype(BF16))
    sk = sub_keys.reshape(2 * PEER_HEADS, PEER_KEYS, PEER_HALF).astype(BF16)
    ns, e1, gt = peer_topk(q, sk)
    return peer_mix(xn, u.astype(BF16), v.T.astype(BF16), ns, e1, gt, hn)


def kernel(x, sb_norm, sb_w_qkv, sb_w_o, sw_norm, sw_w_qkv, sw_q_gain, sw_k_gain, sw_sinks, sw_w_o,
           ffn_norm, peer_w_query, peer_sub_keys, peer_u, peer_v):
    batch, seq, d = x.shape
    t = batch * seq
    h = x.reshape(t, d)

    qkv = norm_matmul(h, sb_norm[0], sb_w_qkv[0].astype(BF16), BF16)
    o = sb_attention(qkv.reshape(batch, seq, -1), batch, seq).reshape(t, -1)
    h = _peer_block(o, sb_w_o[0], h, ffn_norm[0], peer_w_query[0], peer_sub_keys[0],
                    peer_u[0], peer_v[0])

    qkv = norm_matmul(h, sw_norm[0], _sw_qkv_weight(sw_w_qkv[0]).astype(BF16), F32)
    o = sw_attention(qkv.reshape(batch, seq, -1), sw_sinks[0], sw_q_gain[0], sw_k_gain[0],
                     batch, seq).reshape(t, -1)
    h = _peer_block(o, sw_w_o[0], h, ffn_norm[1], peer_w_query[1], peer_sub_keys[1],
                    peer_u[1], peer_v[1])
    return h.reshape(batch, seq, d)
```

```python
import functools

import numpy as np
import jax
import jax.numpy as jnp
from jax import lax
from jax.experimental import pallas as pl
from jax.experimental.pallas import tpu as pltpu

F32 = jnp.float32
BF16 = jnp.bfloat16

D_MODEL = 1024
HEAD_DIM = 64
SB_HEADS = 16
SW_Q_HEADS = 16
SW_KV_HEADS = 4
BLOCK = 128
LANES = 128
PEER_HEADS = 8
PEER_KEYS = 128
PEER_HALF = 128
PEER_TOPK = 16
RMS_EPS = 1e-6
ATTN_SCALE = HEAD_DIM ** -0.5
SB_DEAD_LOG = -110.0
SB_PAIRS = 2
NEG_INF = float("-inf")
VMEM_LIMIT = 48 * 1024 * 1024

_NT = (((1,), (1,)), ((), ()))


def _params(*sem, flags=None):
    return pltpu.CompilerParams(dimension_semantics=sem, vmem_limit_bytes=VMEM_LIMIT, flags=flags)


def _rms_scale(x):
    return lax.rsqrt(jnp.mean(x * x, axis=-1, keepdims=True) + RMS_EPS)


def _norm_matmul_kernel(x_ref, g_ref, w_ref, o_ref):
    x = x_ref[...]
    xn = (x * _rms_scale(x) * g_ref[...]).astype(BF16)
    o_ref[...] = jnp.dot(xn, w_ref[...], preferred_element_type=F32).astype(o_ref.dtype)


def norm_matmul(x, g, w, out_dtype, tm=512):
    t, d = x.shape
    n = w.shape[1]
    tm = min(tm, t)
    return pl.pallas_call(
        _norm_matmul_kernel,
        grid=(t // tm,),
        in_specs=[pl.BlockSpec((tm, d), lambda i: (i, 0)),
                  pl.BlockSpec((1, d), lambda i: (0, 0)),
                  pl.BlockSpec((d, n), lambda i: (0, 0))],
        out_specs=pl.BlockSpec((tm, n), lambda i: (i, 0)),
        out_shape=jax.ShapeDtypeStruct((t, n), out_dtype),
        compiler_params=_params("parallel"),
        name="norm_matmul",
    )(x, g.reshape(1, d), w)


def _sb_kernel(q_ref, k_ref, v_ref, tri1_ref, tri2_ref, o_ref):
    i = pl.program_id(2)
    tq = BLOCK
    nchain = 4 * SB_PAIRS
    lane = lax.broadcasted_iota(jnp.int32, (tq, LANES), 1)
    iota = lambda w, d: lax.broadcasted_iota(jnp.int32, (tq, w), d)
    causal1 = iota(tq, 1) < iota(tq, 0)
    causal2 = iota(2 * tq, 1) < iota(2 * tq, 0) + tq

    def spans(chains, start, w, carries, accs, mask):
        ls = lambda c: slice((c // 4) * LANES, (c // 4 + 1) * LANES)
        tri = tri1_ref[...] if w == tq else tri2_ref[...]
        zs = [lax.dot_general(qs[c], k_ref[0, pl.ds(start, w), ls(c)], _NT,
                              preferred_element_type=F32) for c in chains]
        lfs = [-(jnp.maximum(z, 0.0) + jnp.log1p(jnp.exp(-jnp.abs(z)))) for z in zs]
        wts = []
        for lf in lfs:
            lfm = lf if mask is None else jnp.where(mask, lf, 0.0)
            hi = lfm.astype(BF16)
            lo = (lfm - hi.astype(F32)).astype(BF16)
            wts.append(jnp.dot(jnp.concatenate([hi, lo], axis=1), tri, preferred_element_type=F32))
        out_c, out_a = [], []
        for n, c in enumerate(chains):
            carry = carries[n]
            cw = carry if w == tq else jnp.concatenate([carry, carry], axis=1)
            a = jnp.exp(zs[n] + lfs[n] + wts[n][:, :w] + cw)
            if mask is not None:
                a = jnp.where(mask, a, 0.0)
            out_a.append(accs[n] + jnp.dot(a.astype(BF16), v_ref[0, pl.ds(start, w), ls(c)],
                                           preferred_element_type=F32))
            out_c.append(carry + wts[n][:, w:])
        return out_c, out_a

    qs = []
    for pp in range(SB_PAIRS):
        for u in range(2):
            q2 = q_ref[0, u * tq:(u + 1) * tq, pp * LANES:(pp + 1) * LANES]
            q2 = (q2.astype(F32) * ATTN_SCALE).astype(BF16)
            for hh in range(2):
                hmask = (lane < HEAD_DIM) if hh == 0 else (lane >= HEAD_DIM)
                qs.append(jnp.where(hmask, q2, jnp.zeros_like(q2)))
    zero = jnp.zeros((tq, LANES), F32)
    base = pl.multiple_of(i * (2 * tq), 2 * tq)
    first = [c for c in range(nchain) if (c // 2) % 2 == 0]
    second = [c for c in range(nchain) if (c // 2) % 2 == 1]
    zeros = [zero] * len(first)
    c1, a1 = spans(first, base, tq, zeros, zeros, causal1)
    c2, a2 = spans(second, base, 2 * tq, zeros, zeros, causal2)
    carries, accs = [None] * nchain, [None] * nchain
    for n, c in enumerate(first):
        carries[c], accs[c] = c1[n], a1[n]
    for n, c in enumerate(second):
        carries[c], accs[c] = c2[n], a2[n]

    def alive(cs):
        m = cs[0]
        for x in cs[1:]:
            m = jnp.maximum(m, x)
        return jnp.max(m)

    def cond(st):
        r, _, _, m = st
        return jnp.logical_and(r < i, m > SB_DEAD_LOG)

    def body(st):
        r, cs, ac, _ = st
        start = pl.multiple_of((i - 1 - r) * (2 * tq), 2 * tq)
        cs, ac = spans(list(range(nchain)), start, 2 * tq, list(cs), list(ac), None)
        return r + 1, tuple(cs), tuple(ac), alive(cs)

    _, _, accs, _ = lax.while_loop(
        cond, body, (jnp.int32(0), tuple(carries), tuple(accs), alive(carries)))
    for pp in range(SB_PAIRS):
        for u in range(2):
            c = 4 * pp + 2 * u
            o_ref[0, u * tq:(u + 1) * tq, pp * LANES:(pp + 1) * LANES] = jnp.where(
                lane < HEAD_DIM, accs[c], accs[c + 1]).astype(o_ref.dtype)


def _sb_tri(w):
    r = np.arange(2 * w)[:, None] % w
    c = np.arange(w + LANES)[None, :]
    return jnp.asarray(np.where(c < w, r > c, True), dtype=BF16)


def sb_attention(qkv, batch, seq):
    ngroup = SB_HEADS * HEAD_DIM // (SB_PAIRS * LANES)
    width = SB_PAIRS * LANES
    return pl.pallas_call(
        _sb_kernel,
        grid=(batch, ngroup, seq // (2 * BLOCK)),
        in_specs=[pl.BlockSpec((1, 2 * BLOCK, width), lambda b, p, i: (b, i, p)),
                  pl.BlockSpec((1, seq, width), lambda b, p, i: (b, 0, ngroup + p)),
                  pl.BlockSpec((1, seq, width), lambda b, p, i: (b, 0, 2 * ngroup + p)),
                  pl.BlockSpec((2 * BLOCK, 2 * BLOCK), lambda b, p, i: (0, 0)),
                  pl.BlockSpec((4 * BLOCK, 3 * BLOCK), lambda b, p, i: (0, 0))],
        out_specs=pl.BlockSpec((1, 2 * BLOCK, width), lambda b, p, i: (b, i, p)),
        out_shape=jax.ShapeDtypeStruct((batch, seq, SB_HEADS * HEAD_DIM), BF16),
        compiler_params=_params("parallel", "parallel", "arbitrary"),
        name="sb_attention",
    )(qkv, qkv, qkv, _sb_tri(BLOCK), _sb_tri(2 * BLOCK))


def _alibi_slopes(n_heads):
    return [float(v) for v in np.asarray(
        2.0 ** (-8.0 * np.arange(1, n_heads + 1) / n_heads), dtype=np.float32)]


def _sw_kernel(sink_ref, q_ref, kp_ref, kc_ref, vp_ref, vc_ref, qg_ref, kg_ref, bd_ref, o_ref):
    n = pl.program_id(1)
    lane = lax.broadcasted_iota(jnp.int32, (BLOCK, LANES), 1)
    ii = lax.broadcasted_iota(jnp.int32, (BLOCK, 2 * BLOCK), 0)
    jj = lax.broadcasted_iota(jnp.int32, (BLOCK, 2 * BLOCK), 1)
    dist = BLOCK + ii - jj
    valid = (dist >= 0) & (dist < BLOCK) & jnp.logical_or(jj >= BLOCK, n > 0)
    distf = dist.astype(F32)
    slopes = _alibi_slopes(SW_Q_HEADS)
    bd = bd_ref[...]
    group = SW_Q_HEADS // SW_KV_HEADS
    for kk in range(SW_KV_HEADS):
        cs = slice(kk * LANES, (kk + 1) * LANES)
        k2 = jnp.concatenate([kp_ref[0, :, cs], kc_ref[0, :, cs]], axis=0)
        kn = (k2 * _rms_scale(k2) * kg_ref[...]).astype(BF16)
        v2 = jnp.concatenate([vp_ref[0, :, cs], vc_ref[0, :, cs]], axis=0).astype(BF16)
        for gp in range(group // 2):
            pair = kk * (group // 2) + gp
            qq = q_ref[0, :, pair * LANES:(pair + 1) * LANES]
            sq = qq * qq
            hi = sq.astype(BF16)
            lo = (sq - hi.astype(F32)).astype(BF16)
            ssq = (jnp.dot(hi, bd, preferred_element_type=F32)
                   + jnp.dot(lo, bd, preferred_element_type=F32))
            qn = qq * lax.rsqrt(ssq * (1.0 / HEAD_DIM) + RMS_EPS) * qg_ref[...]
            qn = (qn * ATTN_SCALE).astype(BF16)
            outs = []
            for e in range(2):
                head = 2 * pair + e
                hmask = (lane < HEAD_DIM) if e == 0 else (lane >= HEAD_DIM)
                qh = jnp.where(hmask, qn, jnp.zeros_like(qn))
                s = lax.dot_general(qh, kn, _NT, preferred_element_type=F32)
                s = jnp.where(valid, s - slopes[head] * distf, NEG_INF)
                sink = sink_ref[head]
                m = jnp.maximum(jnp.max(s, axis=-1, keepdims=True), sink)
                p = jnp.exp(s - m)
                denom = jnp.sum(p, axis=-1, keepdims=True) + jnp.exp(sink - m)
                o = jnp.dot(p.astype(BF16), v2, preferred_element_type=F32)
                outs.append(o / denom)
            o_ref[0, :, pair * LANES:(pair + 1) * LANES] = jnp.where(
                lane < HEAD_DIM, outs[0], outs[1]).astype(o_ref.dtype)


def sw_attention(qkv, sinks, q_gain, k_gain, batch, seq):
    nb = seq // BLOCK
    qw = SW_Q_HEADS * HEAD_DIM
    kw = SW_KV_HEADS * LANES
    bd = jnp.asarray(np.kron(np.eye(2), np.ones((HEAD_DIM, HEAD_DIM))), dtype=BF16)
    prev = lambda b, n: (b, jnp.maximum(n - 1, 0), qw // kw)
    cur = lambda b, n: (b, n, qw // kw)
    prev_v = lambda b, n: (b, jnp.maximum(n - 1, 0), qw // kw + 1)
    cur_v = lambda b, n: (b, n, qw // kw + 1)
    vec = lambda g: jnp.tile(g.astype(F32), 2).reshape(1, LANES)
    return pl.pallas_call(
        _sw_kernel,
        grid=(batch, nb),
        in_specs=[pl.BlockSpec(memory_space=pltpu.SMEM),
                  pl.BlockSpec((1, BLOCK, qw), lambda b, n: (b, n, 0)),
                  pl.BlockSpec((1, BLOCK, kw), prev),
                  pl.BlockSpec((1, BLOCK, kw), cur),
                  pl.BlockSpec((1, BLOCK, kw), prev_v),
                  pl.BlockSpec((1, BLOCK, kw), cur_v),
                  pl.BlockSpec((1, LANES), lambda b, n: (0, 0)),
                  pl.BlockSpec((1, LANES), lambda b, n: (0, 0)),
                  pl.BlockSpec((LANES, LANES), lambda b, n: (0, 0))],
        out_specs=pl.BlockSpec((1, BLOCK, qw), lambda b, n: (b, n, 0)),
        out_shape=jax.ShapeDtypeStruct((batch, seq, qw), BF16),
        compiler_params=_params("parallel", "arbitrary"),
        name="sw_attention",
    )(sinks.astype(F32), qkv, qkv, qkv, qkv, qkv, vec(q_gain), vec(k_gain), bd)


def _sw_qkv_weight(w):
    d = w.shape[0]
    qw = SW_Q_HEADS * HEAD_DIM
    kvw = SW_KV_HEADS * HEAD_DIM
    dup = lambda a: jnp.broadcast_to(
        a.reshape(d, SW_KV_HEADS, 1, HEAD_DIM), (d, SW_KV_HEADS, 2, HEAD_DIM)).reshape(d, 2 * kvw)
    return jnp.concatenate([w[:, :qw], dup(w[:, qw:qw + kvw]), dup(w[:, qw + kvw:])], axis=1)


def _proj_kernel(o_ref, wo_ref, h_ref, g_ref, wq_ref, hn_ref, xn_ref, q_ref):
    hn = h_ref[...] + jnp.dot(o_ref[...], wo_ref[...], preferred_element_type=F32)
    hn_ref[...] = hn
    xn = (hn * _rms_scale(hn) * g_ref[...]).astype(BF16)
    xn_ref[...] = xn
    q_ref[...] = jnp.dot(xn, wq_ref[...], preferred_element_type=F32).astype(BF16)


def proj_residual_query(o, wo, h, g, wq, tm=512):
    t, d = h.shape
    nq = wq.shape[1]
    tm = min(tm, t)
    row = lambda i: (i, 0)
    fixed = lambda i: (0, 0)
    return pl.pallas_call(
        _proj_kernel,
        grid=(t // tm,),
        in_specs=[pl.BlockSpec((tm, d), row), pl.BlockSpec((d, d), fixed),
                  pl.BlockSpec((tm, d), row), pl.BlockSpec((1, d), fixed),
                  pl.BlockSpec((d, nq), fixed)],
        out_specs=[pl.BlockSpec((tm, d), row), pl.BlockSpec((tm, d), row),
                   pl.BlockSpec((tm, nq), row)],
        out_shape=[jax.ShapeDtypeStruct((t, d), F32), jax.ShapeDtypeStruct((t, d), BF16),
                   jax.ShapeDtypeStruct((t, nq), BF16)],
        compiler_params=_params("parallel"),
        name="proj_residual_query",
    )(o, wo, h, g.reshape(1, d), wq)


def _cand_tables():
    k = PEER_TOPK
    flat = np.zeros((80,), np.float32)
    neg = np.zeros((80,), np.float32)
    for a in range(k):
        flat[a] = a * k
    for b in range(1, 8):
        for a in range(8):
            r = 16 + 8 * (b - 1) + a
            flat[r] = a * k + b
            if (a + 1) * (b + 1) > k:
                neg[r] = NEG_INF
    for b in range(8, k):
        flat[72 + b - 8] = b
    tile = lambda v: jnp.asarray(np.repeat(v[:, None], LANES, axis=1))
    return tile(flat), tile(neg)


def _dup_words(x):
    b = lax.bitcast_convert_type(x.astype(BF16).astype(F32), jnp.uint32)
    return b | (b >> 16)


def _pair_words(x, stage):
    half = x.shape[0] // 2
    stage[...] = x.astype(BF16).astype(F32)
    ev = lax.bitcast_convert_type(stage[pl.ds(0, half, stride=2), :], jnp.uint32)
    od = lax.bitcast_convert_type(stage[pl.ds(1, half, stride=2), :], jnp.uint32)
    return (ev >> 16) | (od & jnp.uint32(0xFFFF0000))


def _topk_kernel(q_ref, sk_ref, flat_ref, neg_ref, ns_ref, e1_ref, gt_ref,
                 s_scr, rk_scr, tv_scr, cand_scr, sel_scr, stage):
    tt = LANES
    nl = 2 * PEER_HEADS
    iota = lax.broadcasted_iota(jnp.int32, (PEER_KEYS, tt), 0).astype(F32)
    big = float(4 * PEER_KEYS * PEER_KEYS)
    unranked = float(PEER_TOPK)

    def scores(l):
        return lax.dot_general(sk_ref[l], q_ref[:, l * PEER_HALF:(l + 1) * PEER_HALF],
                               _NT, preferred_element_type=F32)

    def l1_reset(l):
        s_scr[l] = scores(l)
        rk_scr[l] = jnp.full((PEER_KEYS, tt), unranked, F32)

    def l1_round(l, r, exact):
        s = s_scr[l]
        m = jnp.max(s, axis=0, keepdims=True)
        hit = s == m
        if exact:
            hit = iota == jnp.min(jnp.where(hit, iota, big), axis=0, keepdims=True)
        s_scr[l] = jnp.where(hit, NEG_INF, s)
        rk_scr[l] = jnp.where(hit, jnp.asarray(r, F32), rk_scr[l])
        tv_scr[l, pl.ds(r, 1), :] = m

    def miscount(x):
        return jnp.abs(jnp.sum(x, axis=0, keepdims=True) - float(PEER_TOPK))

    def redo_where_miscounted(counts, redo):
        worst = counts[0]
        for c in counts[1:]:
            worst = jnp.maximum(worst, c)

        @pl.when(jnp.max(worst) > 0.0)
        def _():
            for idx, c in enumerate(counts):
                pl.when(jnp.max(c) > 0.0)(functools.partial(redo, idx))

    for l in range(nl):
        l1_reset(l)

    def l1_fast(r, c):
        for l in range(nl):
            l1_round(l, r, False)
        return c

    lax.fori_loop(0, PEER_TOPK, l1_fast, 0)

    def l1_redo(l):
        l1_reset(l)

        def l1_exact(r, c):
            l1_round(l, r, True)
            return c

        lax.fori_loop(0, PEER_TOPK, l1_exact, 0)

    redo_where_miscounted(
        [miscount(jnp.where(rk_scr[l] < unranked, 1.0, 0.0)) for l in range(nl)], l1_redo)

    flat = flat_ref[...]

    def l2_reset(h):
        tv1 = tv_scr[2 * h]
        tv2 = tv_scr[2 * h + 1]
        parts = [tv1 + tv2[0:1, :]]
        for b in range(1, 8):
            parts.append(tv1[0:8, :] + tv2[b:b + 1, :])
        parts.append(tv1[0:1, :] + tv2[8:16, :])
        cand_scr[h] = jnp.concatenate(parts, axis=0) + neg_ref[...]
        sel_scr[h] = jnp.zeros((80, tt), F32)

    def l2_round(h, exact):
        cd = cand_scr[h]
        m = jnp.max(cd, axis=0, keepdims=True)
        hit = cd == m
        if exact:
            hit = flat == jnp.min(jnp.where(hit, flat, big), axis=0, keepdims=True)
        cand_scr[h] = jnp.where(hit, NEG_INF, cd)
        sel_scr[h] = jnp.where(hit, 1.0, sel_scr[h])

    for h in range(PEER_HEADS):
        l2_reset(h)

    def l2_fast(r, c):
        for h in range(PEER_HEADS):
            l2_round(h, False)
        return c

    lax.fori_loop(0, PEER_TOPK, l2_fast, 0)

    def l2_redo(h):
        l2_reset(h)

        def l2_exact(r, c):
            l2_round(h, True)
            return c

        lax.fori_loop(0, PEER_TOPK, l2_exact, 0)

    redo_where_miscounted([miscount(sel_scr[h]) for h in range(PEER_HEADS)], l2_redo)

    row8 = lax.broadcasted_iota(jnp.int32, (8, tt), 0)
    for h in range(PEER_HEADS):
        tv1 = tv_scr[2 * h]
        tv2 = tv_scr[2 * h + 1]
        sel = sel_scr[h]
        x1 = jnp.exp(tv1 - tv1[0:1, :])
        x2 = jnp.exp(tv2 - tv2[0:1, :])
        eparts = [x1 * x2[0:1, :]]
        for b in range(1, 8):
            eparts.append(x1[0:8, :] * x2[b:b + 1, :])
        eparts.append(x1[0:1, :] * x2[8:16, :])
        z = jnp.sum(sel * jnp.concatenate(eparts, axis=0), axis=0, keepdims=True)
        n_lo = sel[0:8, :]
        for b in range(1, 8):
            n_lo = n_lo + sel[16 + 8 * (b - 1):24 + 8 * (b - 1), :]
        n_lo = n_lo + jnp.where(row8 == 0, jnp.sum(sel[72:80, :], axis=0, keepdims=True), 0.0)
        s1 = lax.dot_general(sk_ref[2 * h], q_ref[:, (2 * h) * PEER_HALF:(2 * h + 1) * PEER_HALF],
                             _NT, preferred_element_type=F32)
        s2 = lax.dot_general(sk_ref[2 * h + 1],
                             q_ref[:, (2 * h + 1) * PEER_HALF:(2 * h + 2) * PEER_HALF],
                             _NT, preferred_element_type=F32)
        rank1 = rk_scr[2 * h]
        r2 = rk_scr[2 * h + 1]
        a0 = jnp.sum(sel[0:PEER_TOPK, :], axis=0, keepdims=True)
        ns = jnp.where(rank1 < a0, 1.0, 0.0)
        for a in range(8):
            ns = jnp.where(rank1 == float(a), n_lo[a:a + 1, :], ns)
        ns_ref[h] = _dup_words(ns)
        e1_ref[h] = _dup_words(jnp.exp(s1 - tv1[0:1, :]) / z)
        for w, tab in enumerate((r2, jnp.exp(s2 - tv2[0:1, :]))):
            words = _pair_words(tab, stage)
            for k in range(PEER_KEYS // 16):
                gt_ref[_gt_row(k, h, w):_gt_row(k, h, w) + 8, :] = words[8 * k:8 * k + 8, :]


GT_ROWS = (PEER_KEYS // 16) * PEER_HEADS * 2 * 8


def _gt_row(k, h, w):
    return ((k * PEER_HEADS + h) * 2 + w) * 8


def peer_topk(q, sub_keys):
    t = q.shape[0]
    tt = LANES
    nl = 2 * PEER_HEADS
    flat, neg = _cand_tables()
    out_w = jax.ShapeDtypeStruct((PEER_HEADS, PEER_KEYS, t), jnp.uint32)
    out_g = jax.ShapeDtypeStruct((t // tt * GT_ROWS, tt), jnp.uint32)
    ospec = pl.BlockSpec((PEER_HEADS, PEER_KEYS, tt), lambda i: (0, 0, i))
    return pl.pallas_call(
        _topk_kernel,
        grid=(t // tt,),
        in_specs=[pl.BlockSpec((tt, nl * PEER_HALF), lambda i: (i, 0)),
                  pl.BlockSpec((nl, PEER_KEYS, PEER_HALF), lambda i: (0, 0, 0)),
                  pl.BlockSpec((80, tt), lambda i: (0, 0)),
                  pl.BlockSpec((80, tt), lambda i: (0, 0))],
        out_specs=[ospec, ospec, pl.BlockSpec((GT_ROWS, tt), lambda i: (i, 0))],
        out_shape=[out_w, out_w, out_g],
        scratch_shapes=[pltpu.VMEM((nl, PEER_KEYS, tt), F32),
                        pltpu.VMEM((nl, PEER_KEYS, tt), F32),
                        pltpu.VMEM((nl, PEER_TOPK, tt), F32),
                        pltpu.VMEM((PEER_HEADS, 80, tt), F32),
                        pltpu.VMEM((PEER_HEADS, 80, tt), F32),
                        pltpu.VMEM((PEER_KEYS, tt), F32)],
        compiler_params=_params("parallel"),
        name="peer_topk",
    )(q, sub_keys, flat, neg)


GELU_C = float(np.sqrt(2.0 / np.pi))
A_SLABS = 4


def _peer_kernel(xn_ref, u_ref, vt_ref, ns_ref, e1_ref, gt_ref, h_ref, o_ref,
                 acc_ref, coef_ref, *, et, tt):
    j = pl.program_id(1)
    rows = et // PEER_KEYS

    @pl.when(j == 0)
    def _():
        acc_ref[...] = jnp.zeros_like(acc_ref)

    base = pl.multiple_of(j * rows, rows)
    pk = 16
    zero = jnp.zeros((pk, LANES), BF16)

    def bcast(words, r):
        return pltpu.bitcast(jnp.broadcast_to(words[r:r + 1, :], (pk // 2, LANES)), BF16)

    slab = et // A_SLABS
    hts = [lax.dot_general(u_ref[a * slab:(a + 1) * slab, :], xn_ref[...], _NT,
                           preferred_element_type=F32) for a in range(A_SLABS)]
    for r in range(rows):
        ht = hts[r * PEER_KEYS // slab]
        for c in range(tt // LANES):
            cs = slice(c * LANES, (c + 1) * LANES)
            ns = [bcast(ns_ref[h, pl.ds(base, rows), cs], r) for h in range(PEER_HEADS)]
            e1 = [bcast(e1_ref[h, pl.ds(base, rows), cs], r) for h in range(PEER_HEADS)]
            for k in range(PEER_KEYS // pk):
                g = zero
                for h in range(PEER_HEADS):
                    r2 = pltpu.bitcast(gt_ref[pl.ds(c * GT_ROWS + _gt_row(k, h, 0), 8), :], BF16)
                    e2 = pltpu.bitcast(gt_ref[pl.ds(c * GT_ROWS + _gt_row(k, h, 1), 8), :], BF16)
                    g = g + jnp.where(r2 < ns[h], e2, zero) * e1[h]
                es = slice(r * PEER_KEYS + k * pk, r * PEER_KEYS + (k + 1) * pk)
                off = (r * PEER_KEYS) % slab + k * pk
                x = ht[off:off + pk, cs]
                t = jnp.tanh(x * (GELU_C + (GELU_C * 0.044715) * (x * x))).astype(BF16)
                hx = x.astype(BF16) * 0.5
                coef_ref[es, cs] = g * (hx + hx * t)
    acc_ref[...] += jnp.dot(vt_ref[...], coef_ref[...], preferred_element_type=F32)

    @pl.when(j == pl.num_programs(1) - 1)
    def _():
        o_ref[...] = h_ref[...] + acc_ref[...].T


def peer_mix(xn, u, vt, ns, e1, gt, h, et=1024, tt=512):
    t, d = h.shape
    ne = u.shape[0]
    tt = min(tt, t)
    assert et % (8 * PEER_KEYS) == 0 and ne % et == 0 and t % tt == 0
    gspec = pl.BlockSpec((PEER_HEADS, PEER_KEYS, tt), lambda i, j: (0, 0, i))
    return pl.pallas_call(
        functools.partial(_peer_kernel, et=et, tt=tt),
        grid=(t // tt, ne // et),
        in_specs=[pl.BlockSpec((tt, d), lambda i, j: (i, 0)),
                  pl.BlockSpec((et, d), lambda i, j: (j, 0)),
                  pl.BlockSpec((d, et), lambda i, j: (0, j)),
                  gspec, gspec,
                  pl.BlockSpec((tt // LANES * GT_ROWS, LANES), lambda i, j: (i, 0)),
                  pl.BlockSpec((tt, d), lambda i, j: (i, 0))],
        out_specs=pl.BlockSpec((tt, d), lambda i, j: (i, 0)),
        out_shape=jax.ShapeDtypeStruct((t, d), F32),
        scratch_shapes=[pltpu.VMEM((d, tt), F32), pltpu.VMEM((et, tt), BF16)],
        compiler_params=_params("parallel", "arbitrary"),
        name="peer_mix",
    )(xn, u, vt, ns, e1, gt, h)


def _peer_block(o, wo, h, g, wq, sub_keys, u, v):
    hn, xn, q = proj_residual_query(o, wo.astype(BF16), h, g, wq.astype(BF16))
    sk = sub_keys.reshape(2 * PEER_HEADS, PEER_KEYS, PEER_HALF).astype(BF16)
    ns, e1, gt = peer_topk(q, sk)
    return peer_mix(xn, u.astype(BF16), v.T.astype(BF16), ns, e1, gt, hn)


def kernel(x, sb_norm, sb_w_qkv, sb_w_o, sw_norm, sw_w_qkv, sw_q_gain, sw_k_gain, sw_sinks, sw_w_o,
           ffn_norm, peer_w_query, peer_sub_keys, peer_u, peer_v):
    batch, seq, d = x.shape
    t = batch * seq
    h = x.reshape(t, d)

    qkv = norm_matmul(h, sb_norm[0], sb_w_qkv[0].astype(BF16), BF16)
    o = sb_attention(qkv.reshape(batch, seq, -1), batch, seq).reshape(t, -1)
    h = _peer_block(o, sb_w_o[0], h, ffn_norm[0], peer_w_query[0], peer_sub_keys[0],
                    peer_u[0], peer_v[0])

    qkv = norm_matmul(h, sw_norm[0], _sw_qkv_weight(sw_w_qkv[0]).astype(BF16), F32)
    o = sw_attention(qkv.reshape(batch, seq, -1), sw_sinks[0], sw_q_gain[0], sw_k_gain[0],
                     batch, seq).reshape(t, -1)
    h = _peer_block(o, sw_w_o[0], h, ffn_norm[1], peer_w_query[1], peer_sub_keys[1],
                    peer_u[1], peer_v[1])
    return h.reshape(batch, seq, d)
```

```python
import functools

import numpy as np
import jax
import jax.numpy as jnp
from jax import lax
from jax.experimental import pallas as pl
from jax.experimental.pallas import tpu as pltpu

F32 = jnp.float32
BF16 = jnp.bfloat16

D_MODEL = 1024
HEAD_DIM = 64
SB_HEADS = 16
SW_Q_HEADS = 16
SW_KV_HEADS = 4
BLOCK = 128
LANES = 128
PEER_HEADS = 8
PEER_KEYS = 128
PEER_HALF = 128
PEER_TOPK = 16
RMS_EPS = 1e-6
ATTN_SCALE = HEAD_DIM ** -0.5
SB_DEAD_LOG = -110.0
SB_PAIRS = 2
NEG_INF = float("-inf")
VMEM_LIMIT = 56 * 1024 * 1024
PEER_ET = 1024

_NT = (((1,), (1,)), ((), ()))


def _params(*sem, flags=None):
    return pltpu.CompilerParams(dimension_semantics=sem, vmem_limit_bytes=VMEM_LIMIT, flags=flags)


def _rms_scale(x):
    return lax.rsqrt(jnp.mean(x * x, axis=-1, keepdims=True) + RMS_EPS)


def _norm_matmul_kernel(x_ref, g_ref, w_ref, o_ref):
    x = x_ref[...]
    xn = (x * _rms_scale(x) * g_ref[...]).astype(BF16)
    o_ref[...] = jnp.dot(xn, w_ref[...], preferred_element_type=F32).astype(o_ref.dtype)


def norm_matmul(x, g, w, out_dtype, tm=512):
    t, d = x.shape
    n = w.shape[1]
    tm = min(tm, t)
    return pl.pallas_call(
        _norm_matmul_kernel,
        grid=(t // tm,),
        in_specs=[pl.BlockSpec((tm, d), lambda i: (i, 0)),
                  pl.BlockSpec((1, d), lambda i: (0, 0)),
                  pl.BlockSpec((d, n), lambda i: (0, 0))],
        out_specs=pl.BlockSpec((tm, n), lambda i: (i, 0)),
        out_shape=jax.ShapeDtypeStruct((t, n), out_dtype),
        compiler_params=_params("parallel"),
        name="norm_matmul",
    )(x, g.reshape(1, d), w)


def _sb_kernel(q_ref, k_ref, v_ref, tri1_ref, tri2_ref, o_ref):
    i = pl.program_id(2)
    tq = BLOCK
    nchain = 4 * SB_PAIRS
    lane = lax.broadcasted_iota(jnp.int32, (tq, LANES), 1)
    iota = lambda w, d: lax.broadcasted_iota(jnp.int32, (tq, w), d)
    causal1 = iota(tq, 1) < iota(tq, 0)
    causal2 = iota(2 * tq, 1) < iota(2 * tq, 0) + tq

    def spans(chains, start, w, carries, accs, mask):
        ls = lambda c: slice((c // 4) * LANES, (c // 4 + 1) * LANES)
        tri = tri1_ref[...] if w == tq else tri2_ref[...]
        zs = [lax.dot_general(qs[c], k_ref[0, pl.ds(start, w), ls(c)], _NT,
                              preferred_element_type=F32) for c in chains]
        lfs = [-(jnp.maximum(z, 0.0) + jnp.log1p(jnp.exp(-jnp.abs(z)))) for z in zs]
        wts = []
        for lf in lfs:
            lfm = lf if mask is None else jnp.where(mask, lf, 0.0)
            hi = lfm.astype(BF16)
            lo = (lfm - hi.astype(F32)).astype(BF16)
            wts.append(jnp.dot(jnp.concatenate([hi, lo], axis=1), tri, preferred_element_type=F32))
        out_c, out_a = [], []
        for n, c in enumerate(chains):
            carry = carries[n]
            cw = carry if w == tq else jnp.concatenate([carry, carry], axis=1)
            a = jnp.exp(zs[n] + lfs[n] + wts[n][:, :w] + cw)
            if mask is not None:
                a = jnp.where(mask, a, 0.0)
            out_a.append(accs[n] + jnp.dot(a.astype(BF16), v_ref[0, pl.ds(start, w), ls(c)],
                                           preferred_element_type=F32))
            out_c.append(carry + wts[n][:, w:])
        return out_c, out_a

    qs = []
    for pp in range(SB_PAIRS):
        for u in range(2):
            q2 = q_ref[0, u * tq:(u + 1) * tq, pp * LANES:(pp + 1) * LANES]
            q2 = (q2.astype(F32) * ATTN_SCALE).astype(BF16)
            for hh in range(2):
                hmask = (lane < HEAD_DIM) if hh == 0 else (lane >= HEAD_DIM)
                qs.append(jnp.where(hmask, q2, jnp.zeros_like(q2)))
    zero = jnp.zeros((tq, LANES), F32)
    base = pl.multiple_of(i * (2 * tq), 2 * tq)
    first = [c for c in range(nchain) if (c // 2) % 2 == 0]
    second = [c for c in range(nchain) if (c // 2) % 2 == 1]
    zeros = [zero] * len(first)
    c1, a1 = spans(first, base, tq, zeros, zeros, causal1)
    c2, a2 = spans(second, base, 2 * tq, zeros, zeros, causal2)
    carries, accs = [None] * nchain, [None] * nchain
    for n, c in enumerate(first):
        carries[c], accs[c] = c1[n], a1[n]
    for n, c in enumerate(second):
        carries[c], accs[c] = c2[n], a2[n]

    def alive(cs):
        m = cs[0]
        for x in cs[1:]:
            m = jnp.maximum(m, x)
        return jnp.max(m)

    def cond(st):
        r, _, _, m = st
        return jnp.logical_and(r < i, m > SB_DEAD_LOG)

    def body(st):
        r, cs, ac, _ = st
        start = pl.multiple_of((i - 1 - r) * (2 * tq), 2 * tq)
        cs, ac = spans(list(range(nchain)), start, 2 * tq, list(cs), list(ac), None)
        return r + 1, tuple(cs), tuple(ac), alive(cs)

    _, _, accs, _ = lax.while_loop(
        cond, body, (jnp.int32(0), tuple(carries), tuple(accs), alive(carries)))
    for pp in range(SB_PAIRS):
        for u in range(2):
            c = 4 * pp + 2 * u
            o_ref[0, u * tq:(u + 1) * tq, pp * LANES:(pp + 1) * LANES] = jnp.where(
                lane < HEAD_DIM, accs[c], accs[c + 1]).astype(o_ref.dtype)


def _sb_tri(w):
    r = np.arange(2 * w)[:, None] % w
    c = np.arange(w + LANES)[None, :]
    return jnp.asarray(np.where(c < w, r > c, True), dtype=BF16)


def sb_attention(qkv, batch, seq):
    ngroup = SB_HEADS * HEAD_DIM // (SB_PAIRS * LANES)
    width = SB_PAIRS * LANES
    return pl.pallas_call(
        _sb_kernel,
        grid=(batch, ngroup, seq // (2 * BLOCK)),
        in_specs=[pl.BlockSpec((1, 2 * BLOCK, width), lambda b, p, i: (b, i, p)),
                  pl.BlockSpec((1, seq, width), lambda b, p, i: (b, 0, ngroup + p)),
                  pl.BlockSpec((1, seq, width), lambda b, p, i: (b, 0, 2 * ngroup + p)),
                  pl.BlockSpec((2 * BLOCK, 2 * BLOCK), lambda b, p, i: (0, 0)),
                  pl.BlockSpec((4 * BLOCK, 3 * BLOCK), lambda b, p, i: (0, 0))],
        out_specs=pl.BlockSpec((1, 2 * BLOCK, width), lambda b, p, i: (b, i, p)),
        out_shape=jax.ShapeDtypeStruct((batch, seq, SB_HEADS * HEAD_DIM), BF16),
        compiler_params=_params("parallel", "parallel", "arbitrary"),
        name="sb_attention",
    )(qkv, qkv, qkv, _sb_tri(BLOCK), _sb_tri(2 * BLOCK))


def _alibi_slopes(n_heads):
    return [float(v) for v in np.asarray(
        2.0 ** (-8.0 * np.arange(1, n_heads + 1) / n_heads), dtype=np.float32)]


def _sw_kernel(sink_ref, q_ref, kp_ref, kc_ref, vp_ref, vc_ref, qg_ref, kg_ref, bd_ref, o_ref):
    n = pl.program_id(1)
    lane = lax.broadcasted_iota(jnp.int32, (BLOCK, LANES), 1)
    ii = lax.broadcasted_iota(jnp.int32, (BLOCK, 2 * BLOCK), 0)
    jj = lax.broadcasted_iota(jnp.int32, (BLOCK, 2 * BLOCK), 1)
    dist = BLOCK + ii - jj
    valid = (dist >= 0) & (dist < BLOCK) & jnp.logical_or(jj >= BLOCK, n > 0)
    distf = dist.astype(F32)
    slopes = _alibi_slopes(SW_Q_HEADS)
    bd = bd_ref[...]
    group = SW_Q_HEADS // SW_KV_HEADS
    heads = range(SW_Q_HEADS)
    kns, v2s = [], []
    for kk in range(SW_KV_HEADS):
        cs = slice(kk * LANES, (kk + 1) * LANES)
        k2 = jnp.concatenate([kp_ref[0, :, cs], kc_ref[0, :, cs]], axis=0)
        kns.append((k2 * _rms_scale(k2) * kg_ref[...]).astype(BF16))
        v2s.append(jnp.concatenate([vp_ref[0, :, cs], vc_ref[0, :, cs]], axis=0).astype(BF16))
    qns = []
    for pair in range(SW_Q_HEADS // 2):
        qq = q_ref[0, :, pair * LANES:(pair + 1) * LANES]
        sq = qq * qq
        hi = sq.astype(BF16)
        lo = (sq - hi.astype(F32)).astype(BF16)
        ssq = (jnp.dot(hi, bd, preferred_element_type=F32)
               + jnp.dot(lo, bd, preferred_element_type=F32))
        qn = qq * lax.rsqrt(ssq * (1.0 / HEAD_DIM) + RMS_EPS) * qg_ref[...]
        qns.append((qn * ATTN_SCALE).astype(BF16))
    ss = []
    for head in heads:
        qn = qns[head // 2]
        hmask = (lane < HEAD_DIM) if head % 2 == 0 else (lane >= HEAD_DIM)
        qh = jnp.where(hmask, qn, jnp.zeros_like(qn))
        s = lax.dot_general(qh, kns[head // group], _NT, preferred_element_type=F32)
        ss.append(jnp.where(valid, s - slopes[head] * distf, NEG_INF))
    ms = [jnp.maximum(jnp.max(ss[head], axis=-1, keepdims=True), sink_ref[head]) for head in heads]
    ps = [jnp.exp(ss[head] - ms[head]) for head in heads]
    denoms = [jnp.sum(ps[head], axis=-1, keepdims=True) + jnp.exp(sink_ref[head] - ms[head])
              for head in heads]
    outs = [jnp.dot(ps[head].astype(BF16), v2s[head // group], preferred_element_type=F32)
            / denoms[head] for head in heads]
    for pair in range(SW_Q_HEADS // 2):
        o_ref[0, :, pair * LANES:(pair + 1) * LANES] = jnp.where(
            lane < HEAD_DIM, outs[2 * pair], outs[2 * pair + 1]).astype(o_ref.dtype)


def sw_attention(qkv, sinks, q_gain, k_gain, batch, seq):
    nb = seq // BLOCK
    qw = SW_Q_HEADS * HEAD_DIM
    kw = SW_KV_HEADS * LANES
    bd = jnp.asarray(np.kron(np.eye(2), np.ones((HEAD_DIM, HEAD_DIM))), dtype=BF16)
    prev = lambda b, n: (b, jnp.maximum(n - 1, 0), qw // kw)
    cur = lambda b, n: (b, n, qw // kw)
    prev_v = lambda b, n: (b, jnp.maximum(n - 1, 0), qw // kw + 1)
    cur_v = lambda b, n: (b, n, qw // kw + 1)
    vec = lambda g: jnp.tile(g.astype(F32), 2).reshape(1, LANES)
    return pl.pallas_call(
        _sw_kernel,
        grid=(batch, nb),
        in_specs=[pl.BlockSpec(memory_space=pltpu.SMEM),
                  pl.BlockSpec((1, BLOCK, qw), lambda b, n: (b, n, 0)),
                  pl.BlockSpec((1, BLOCK, kw), prev),
                  pl.BlockSpec((1, BLOCK, kw), cur),
                  pl.BlockSpec((1, BLOCK, kw), prev_v),
                  pl.BlockSpec((1, BLOCK, kw), cur_v),
                  pl.BlockSpec((1, LANES), lambda b, n: (0, 0)),
                  pl.BlockSpec((1, LANES), lambda b, n: (0, 0)),
                  pl.BlockSpec((LANES, LANES), lambda b, n: (0, 0))],
        out_specs=pl.BlockSpec((1, BLOCK, qw), lambda b, n: (b, n, 0)),
        out_shape=jax.ShapeDtypeStruct((batch, seq, qw), BF16),
        compiler_params=_params("parallel", "arbitrary"),
        name="sw_attention",
    )(sinks.astype(F32), qkv, qkv, qkv, qkv, qkv, vec(q_gain), vec(k_gain), bd)


def _sw_qkv_weight(w):
    d = w.shape[0]
    qw = SW_Q_HEADS * HEAD_DIM
    kvw = SW_KV_HEADS * HEAD_DIM
    dup = lambda a: jnp.broadcast_to(
        a.reshape(d, SW_KV_HEADS, 1, HEAD_DIM), (d, SW_KV_HEADS, 2, HEAD_DIM)).reshape(d, 2 * kvw)
    return jnp.concatenate([w[:, :qw], dup(w[:, qw:qw + kvw]), dup(w[:, qw + kvw:])], axis=1)


def _proj_kernel(o_ref, wo_ref, h_ref, g_ref, wq_ref, hn_ref, xn_ref, q_ref):
    hn = h_ref[...] + jnp.dot(o_ref[...], wo_ref[...], preferred_element_type=F32)
    hn_ref[...] = hn
    xn = (hn * _rms_scale(hn) * g_ref[...]).astype(BF16)
    xn_ref[...] = xn
    q_ref[...] = jnp.dot(xn, wq_ref[...], preferred_element_type=F32).astype(BF16)


def proj_residual_query(o, wo, h, g, wq, tm=512):
    t, d = h.shape
    nq = wq.shape[1]
    tm = min(tm, t)
    row = lambda i: (i, 0)
    fixed = lambda i: (0, 0)
    return pl.pallas_call(
        _proj_kernel,
        grid=(t // tm,),
        in_specs=[pl.BlockSpec((tm, d), row), pl.BlockSpec((d, d), fixed),
                  pl.BlockSpec((tm, d), row), pl.BlockSpec((1, d), fixed),
                  pl.BlockSpec((d, nq), fixed)],
        out_specs=[pl.BlockSpec((tm, d), row), pl.BlockSpec((tm, d), row),
                   pl.BlockSpec((tm, nq), row)],
        out_shape=[jax.ShapeDtypeStruct((t, d), F32), jax.ShapeDtypeStruct((t, d), BF16),
                   jax.ShapeDtypeStruct((t, nq), BF16)],
        compiler_params=_params("parallel"),
        name="proj_residual_query",
    )(o, wo, h, g.reshape(1, d), wq)


def _cand_tables():
    k = PEER_TOPK
    flat = np.zeros((80,), np.float32)
    neg = np.zeros((80,), np.float32)
    for a in range(k):
        flat[a] = a * k
    for b in range(1, 8):
        for a in range(8):
            r = 16 + 8 * (b - 1) + a
            flat[r] = a * k + b
            if (a + 1) * (b + 1) > k:
                neg[r] = NEG_INF
    for b in range(8, k):
        flat[72 + b - 8] = b
    tile = lambda v: jnp.asarray(np.repeat(v[:, None], LANES, axis=1))
    return tile(flat), tile(neg)


def _dup_words(x):
    b = lax.bitcast_convert_type(x.astype(BF16).astype(F32), jnp.uint32)
    return b | (b >> 16)


def _pair_words(x, stage):
    half = x.shape[0] // 2
    stage[...] = x.astype(BF16).astype(F32)
    ev = lax.bitcast_convert_type(stage[pl.ds(0, half, stride=2), :], jnp.uint32)
    od = lax.bitcast_convert_type(stage[pl.ds(1, half, stride=2), :], jnp.uint32)
    return (ev >> 16) | (od & jnp.uint32(0xFFFF0000))


def _topk_kernel(q_ref, sk_ref, flat_ref, neg_ref, ns_ref, e1_ref, gt_ref,
                 s_scr, rk_scr, tv_scr, cand_scr, sel_scr, stage):
    tt = LANES
    nl = 2 * PEER_HEADS
    iota = lax.broadcasted_iota(jnp.int32, (PEER_KEYS, tt), 0).astype(F32)
    big = float(4 * PEER_KEYS * PEER_KEYS)
    unranked = float(PEER_TOPK)

    def scores(l):
        return lax.dot_general(sk_ref[l], q_ref[:, l * PEER_HALF:(l + 1) * PEER_HALF],
                               _NT, preferred_element_type=F32)

    def l1_reset(l):
        s_scr[l] = scores(l)
        rk_scr[l] = jnp.full((PEER_KEYS, tt), unranked, F32)

    def l1_round(l, r, exact):
        s = s_scr[l]
        m = jnp.max(s, axis=0, keepdims=True)
        hit = s == m
        if exact:
            hit = iota == jnp.min(jnp.where(hit, iota, big), axis=0, keepdims=True)
        s_scr[l] = jnp.where(hit, NEG_INF, s)
        rk_scr[l] = jnp.where(hit, jnp.asarray(r, F32), rk_scr[l])
        tv_scr[l, pl.ds(r, 1), :] = m

    def miscount(x):
        return jnp.abs(jnp.sum(x, axis=0, keepdims=True) - float(PEER_TOPK))

    def redo_where_miscounted(counts, redo):
        worst = counts[0]
        for c in counts[1:]:
            worst = jnp.maximum(worst, c)

        @pl.when(jnp.max(worst) > 0.0)
        def _():
            for idx, c in enumerate(counts):
                pl.when(jnp.max(c) > 0.0)(functools.partial(redo, idx))

    for l in range(nl):
        l1_reset(l)

    def l1_fast(r, c):
        for l in range(nl):
            l1_round(l, r, False)
        return c

    lax.fori_loop(0, PEER_TOPK, l1_fast, 0)

    def l1_redo(l):
        l1_reset(l)

        def l1_exact(r, c):
            l1_round(l, r, True)
            return c

        lax.fori_loop(0, PEER_TOPK, l1_exact, 0)

    redo_where_miscounted(
        [miscount(jnp.where(rk_scr[l] < unranked, 1.0, 0.0)) for l in range(nl)], l1_redo)

    flat = flat_ref[...]

    def l2_reset(h):
        tv1 = tv_scr[2 * h]
        tv2 = tv_scr[2 * h + 1]
        parts = [tv1 + tv2[0:1, :]]
        for b in range(1, 8):
            parts.append(tv1[0:8, :] + tv2[b:b + 1, :])
        parts.append(tv1[0:1, :] + tv2[8:16, :])
        cand_scr[h] = jnp.concatenate(parts, axis=0) + neg_ref[...]
        sel_scr[h] = jnp.zeros((80, tt), F32)

    def l2_round(h, exact):
        cd = cand_scr[h]
        m = jnp.max(cd, axis=0, keepdims=True)
        hit = cd == m
        if exact:
            hit = flat == jnp.min(jnp.where(hit, flat, big), axis=0, keepdims=True)
        cand_scr[h] = jnp.where(hit, NEG_INF, cd)
        sel_scr[h] = jnp.where(hit, 1.0, sel_scr[h])

    for h in range(PEER_HEADS):
        l2_reset(h)

    def l2_fast(r, c):
        for h in range(PEER_HEADS):
            l2_round(h, False)
        return c

    lax.fori_loop(0, PEER_TOPK, l2_fast, 0)

    def l2_redo(h):
        l2_reset(h)

        def l2_exact(r, c):
            l2_round(h, True)
            return c

        lax.fori_loop(0, PEER_TOPK, l2_exact, 0)

    redo_where_miscounted([miscount(sel_scr[h]) for h in range(PEER_HEADS)], l2_redo)

    row8 = lax.broadcasted_iota(jnp.int32, (8, tt), 0)
    for h in range(PEER_HEADS):
        tv1 = tv_scr[2 * h]
        tv2 = tv_scr[2 * h + 1]
        sel = sel_scr[h]
        x1 = jnp.exp(tv1 - tv1[0:1, :])
        x2 = jnp.exp(tv2 - tv2[0:1, :])
        eparts = [x1 * x2[0:1, :]]
        for b in range(1, 8):
            eparts.append(x1[0:8, :] * x2[b:b + 1, :])
        eparts.append(x1[0:1, :] * x2[8:16, :])
        z = jnp.sum(sel * jnp.concatenate(eparts, axis=0), axis=0, keepdims=True)
        n_lo = sel[0:8, :]
        for b in range(1, 8):
            n_lo = n_lo + sel[16 + 8 * (b - 1):24 + 8 * (b - 1), :]
        n_lo = n_lo + jnp.where(row8 == 0, jnp.sum(sel[72:80, :], axis=0, keepdims=True), 0.0)
        s1 = lax.dot_general(sk_ref[2 * h], q_ref[:, (2 * h) * PEER_HALF:(2 * h + 1) * PEER_HALF],
                             _NT, preferred_element_type=F32)
        s2 = lax.dot_general(sk_ref[2 * h + 1],
                             q_ref[:, (2 * h + 1) * PEER_HALF:(2 * h + 2) * PEER_HALF],
                             _NT, preferred_element_type=F32)
        rank1 = rk_scr[2 * h]
        r2 = rk_scr[2 * h + 1]
        a0 = jnp.sum(sel[0:PEER_TOPK, :], axis=0, keepdims=True)
        ns = jnp.where(rank1 < a0, 1.0, 0.0)
        for a in range(8):
            ns = jnp.where(rank1 == float(a), n_lo[a:a + 1, :], ns)
        ns_ref[h] = _dup_words(ns)
        e1_ref[h] = _dup_words(jnp.exp(s1 - tv1[0:1, :]) / z)
        for w, tab in enumerate((r2, jnp.exp(s2 - tv2[0:1, :]))):
            words = _pair_words(tab, stage)
            for k in range(PEER_KEYS // 16):
                gt_ref[_gt_row(k, h, w):_gt_row(k, h, w) + 8, :] = words[8 * k:8 * k + 8, :]


GT_ROWS = (PEER_KEYS // 16) * PEER_HEADS * 2 * 8


def _gt_row(k, h, w):
    return ((k * PEER_HEADS + h) * 2 + w) * 8


def peer_topk(q, sub_keys):
    t = q.shape[0]
    tt = LANES
    nl = 2 * PEER_HEADS
    flat, neg = _cand_tables()
    out_w = jax.ShapeDtypeStruct((PEER_HEADS, PEER_KEYS, t), jnp.uint32)
    out_g = jax.ShapeDtypeStruct((t // tt * GT_ROWS, tt), jnp.uint32)
    ospec = pl.BlockSpec((PEER_HEADS, PEER_KEYS, tt), lambda i: (0, 0, i))
    return pl.pallas_call(
        _topk_kernel,
        grid=(t // tt,),
        in_specs=[pl.BlockSpec((tt, nl * PEER_HALF), lambda i: (i, 0)),
                  pl.BlockSpec((nl, PEER_KEYS, PEER_HALF), lambda i: (0, 0, 0)),
                  pl.BlockSpec((80, tt), lambda i: (0, 0)),
                  pl.BlockSpec((80, tt), lambda i: (0, 0))],
        out_specs=[ospec, ospec, pl.BlockSpec((GT_ROWS, tt), lambda i: (i, 0))],
        out_shape=[out_w, out_w, out_g],
        scratch_shapes=[pltpu.VMEM((nl, PEER_KEYS, tt), F32),
                        pltpu.VMEM((nl, PEER_KEYS, tt), F32),
                        pltpu.VMEM((nl, PEER_TOPK, tt), F32),
                        pltpu.VMEM((PEER_HEADS, 80, tt), F32),
                        pltpu.VMEM((PEER_HEADS, 80, tt), F32),
                        pltpu.VMEM((PEER_KEYS, tt), F32)],
        compiler_params=_params("parallel"),
        name="peer_topk",
    )(q, sub_keys, flat, neg)


GELU_C = float(np.sqrt(2.0 / np.pi))
A_SLABS = 4


def _peer_kernel(xn_ref, u_ref, vt_ref, ns_ref, e1_ref, gt_ref, h_ref, o_ref,
                 acc_ref, coef_ref, *, et, tt):
    j = pl.program_id(1)
    rows = et // PEER_KEYS

    @pl.when(j == 0)
    def _():
        acc_ref[...] = jnp.zeros_like(acc_ref)

    base = pl.multiple_of(j * rows, rows)
    pk = 16
    zero = jnp.zeros((pk, LANES), BF16)

    def bcast(words, r):
        return pltpu.bitcast(jnp.broadcast_to(words[r:r + 1, :], (pk // 2, LANES)), BF16)

    slab = et // A_SLABS
    hts = [lax.dot_general(u_ref[a * slab:(a + 1) * slab, :], xn_ref[...], _NT,
                           preferred_element_type=F32) for a in range(A_SLABS)]
    for r in range(rows):
        ht = hts[r * PEER_KEYS // slab]
        for c in range(tt // LANES):
            cs = slice(c * LANES, (c + 1) * LANES)
            ns = [bcast(ns_ref[h, pl.ds(base, rows), cs], r) for h in range(PEER_HEADS)]
            e1 = [bcast(e1_ref[h, pl.ds(base, rows), cs], r) for h in range(PEER_HEADS)]
            for k in range(PEER_KEYS // pk):
                g = zero
                for h in range(PEER_HEADS):
                    r2 = pltpu.bitcast(gt_ref[pl.ds(c * GT_ROWS + _gt_row(k, h, 0), 8), :], BF16)
                    e2 = pltpu.bitcast(gt_ref[pl.ds(c * GT_ROWS + _gt_row(k, h, 1), 8), :], BF16)
                    g = g + jnp.where(r2 < ns[h], e2, zero) * e1[h]
                es = slice(r * PEER_KEYS + k * pk, r * PEER_KEYS + (k + 1) * pk)
                off = (r * PEER_KEYS) % slab + k * pk
                x = ht[off:off + pk, cs]
                t = jnp.tanh(x * (GELU_C + (GELU_C * 0.044715) * (x * x))).astype(BF16)
                hx = x.astype(BF16) * 0.5
                coef_ref[es, cs] = g * (hx + hx * t)
    acc_ref[...] += jnp.dot(vt_ref[0], coef_ref[...], preferred_element_type=F32)

    @pl.when(j == pl.num_programs(1) - 1)
    def _():
        o_ref[...] = h_ref[...] + acc_ref[...].T


def peer_mix(xn, u, vt, ns, e1, gt, h, et=PEER_ET, tt=1024):
    t, d = h.shape
    ne = u.shape[0]
    tt = min(tt, t)
    assert et % (8 * PEER_KEYS) == 0 and ne % et == 0 and t % tt == 0
    once = pl.Buffered(1)
    gspec = pl.BlockSpec((PEER_HEADS, PEER_KEYS, tt), lambda i, j: (0, 0, i), pipeline_mode=once)
    return pl.pallas_call(
        functools.partial(_peer_kernel, et=et, tt=tt),
        grid=(t // tt, ne // et),
        in_specs=[pl.BlockSpec((tt, d), lambda i, j: (i, 0), pipeline_mode=once),
                  pl.BlockSpec((et, d), lambda i, j: (j, 0)),
                  pl.BlockSpec((1, d, et), lambda i, j: (j, 0, 0)),
                  gspec, gspec,
                  pl.BlockSpec((tt // LANES * GT_ROWS, LANES), lambda i, j: (i, 0), pipeline_mode=once),
                  pl.BlockSpec((tt, d), lambda i, j: (i, 0), pipeline_mode=once)],
        out_specs=pl.BlockSpec((tt, d), lambda i, j: (i, 0)),
        out_shape=jax.ShapeDtypeStruct((t, d), F32),
        scratch_shapes=[pltpu.VMEM((d, tt), F32), pltpu.VMEM((et, tt), BF16)],
        compiler_params=_params("parallel", "arbitrary"),
        name="peer_mix",
    )(xn, u, vt, ns, e1, gt, h)


def _vt_tiles(v, et=PEER_ET):
    ne, d = v.shape
    return v.astype(BF16).reshape(ne // et, et, d).transpose(0, 2, 1)


def _peer_block(o, wo, h, g, wq, sub_keys, u, v):
    hn, xn, q = proj_residual_query(o, wo.astype(BF16), h, g, wq.astype(BF16))
    sk = sub_keys.reshape(2 * PEER_HEADS, PEER_KEYS, PEER_HALF).astype(BF16)
    ns, e1, gt = peer_topk(q, sk)
    return peer_mix(xn, u.astype(BF16), _vt_tiles(v), ns, e1, gt, hn)


def kernel(x, sb_norm, sb_w_qkv, sb_w_o, sw_norm, sw_w_qkv, sw_q_gain, sw_k_gain, sw_sinks, sw_w_o,
           ffn_norm, peer_w_query, peer_sub_keys, peer_u, peer_v):
    batch, seq, d = x.shape
    t = batch * seq
    h = x.reshape(t, d)

    qkv = norm_matmul(h, sb_norm[0], sb_w_qkv[0].astype(BF16), BF16)
    o = sb_attention(qkv.reshape(batch, seq, -1), batch, seq).reshape(t, -1)
    h = _peer_block(o, sb_w_o[0], h, ffn_norm[0], peer_w_query[0], peer_sub_keys[0],
                    peer_u[0], peer_v[0])

    qkv = norm_matmul(h, sw_norm[0], _sw_qkv_weight(sw_w_qkv[0]).astype(BF16), F32)
    o = sw_attention(qkv.reshape(batch, seq, -1), sw_sinks[0], sw_q_gain[0], sw_k_gain[0],
                     batch, seq).reshape(t, -1)
    h = _peer_block(o, sw_w_o[0], h, ffn_norm[1], peer_w_query[1], peer_sub_keys[1],
                    peer_u[1], peer_v[1])
    return h.reshape(batch, seq, d)
```

```python
import functools

import numpy as np
import jax
import jax.numpy as jnp
from jax import lax
from jax.experimental import pallas as pl
from jax.experimental.pallas import tpu as pltpu

F32 = jnp.float32
BF16 = jnp.bfloat16

D_MODEL = 1024
HEAD_DIM = 64
SB_HEADS = 16
SW_Q_HEADS = 16
SW_KV_HEADS = 4
BLOCK = 128
LANES = 128
PEER_HEADS = 8
PEER_KEYS = 128
PEER_HALF = 128
PEER_TOPK = 16
RMS_EPS = 1e-6
ATTN_SCALE = HEAD_DIM ** -0.5
SB_DEAD_LOG = -110.0
SB_PAIRS = 2
NEG_INF = float("-inf")
VMEM_LIMIT = 56 * 1024 * 1024
PEER_ET = 1024

_NT = (((1,), (1,)), ((), ()))


def _params(*sem, flags=None):
    return pltpu.CompilerParams(dimension_semantics=sem, vmem_limit_bytes=VMEM_LIMIT, flags=flags)


def _rms_scale(x):
    return lax.rsqrt(jnp.mean(x * x, axis=-1, keepdims=True) + RMS_EPS)


def _norm_matmul_kernel(x_ref, g_ref, w_ref, o_ref):
    x = x_ref[...]
    xn = (x * _rms_scale(x) * g_ref[...]).astype(BF16)
    o_ref[...] = jnp.dot(xn, w_ref[...], preferred_element_type=F32).astype(o_ref.dtype)


def norm_matmul(x, g, w, out_dtype, tm=512):
    t, d = x.shape
    n = w.shape[1]
    tm = min(tm, t)
    return pl.pallas_call(
        _norm_matmul_kernel,
        grid=(t // tm,),
        in_specs=[pl.BlockSpec((tm, d), lambda i: (i, 0)),
                  pl.BlockSpec((1, d), lambda i: (0, 0)),
                  pl.BlockSpec((d, n), lambda i: (0, 0))],
        out_specs=pl.BlockSpec((tm, n), lambda i: (i, 0)),
        out_shape=jax.ShapeDtypeStruct((t, n), out_dtype),
        compiler_params=_params("parallel"),
        name="norm_matmul",
    )(x, g.reshape(1, d), w)


def _sb_kernel(q_ref, k_ref, v_ref, tri1_ref, tri2_ref, o_ref):
    i = pl.program_id(2)
    tq = BLOCK
    nchain = 4 * SB_PAIRS
    lane = lax.broadcasted_iota(jnp.int32, (tq, LANES), 1)
    iota = lambda w, d: lax.broadcasted_iota(jnp.int32, (tq, w), d)
    causal1 = iota(tq, 1) < iota(tq, 0)
    causal2 = iota(2 * tq, 1) < iota(2 * tq, 0) + tq

    def spans(chains, start, w, carries, accs, mask):
        ls = lambda c: slice((c // 4) * LANES, (c // 4 + 1) * LANES)
        tri = tri1_ref[...] if w == tq else tri2_ref[...]
        zs = [lax.dot_general(qs[c], k_ref[0, pl.ds(start, w), ls(c)], _NT,
                              preferred_element_type=F32) for c in chains]
        lfs = [-(jnp.maximum(z, 0.0) + jnp.log1p(jnp.exp(-jnp.abs(z)))) for z in zs]
        wts = []
        for lf in lfs:
            lfm = lf if mask is None else jnp.where(mask, lf, 0.0)
            hi = lfm.astype(BF16)
            lo = (lfm - hi.astype(F32)).astype(BF16)
            wts.append(jnp.dot(jnp.concatenate([hi, lo], axis=1), tri, preferred_element_type=F32))
        out_c, out_a = [], []
        for n, c in enumerate(chains):
            carry = carries[n]
            cw = carry if w == tq else jnp.concatenate([carry, carry], axis=1)
            a = jnp.exp(zs[n] + lfs[n] + wts[n][:, :w] + cw)
            if mask is not None:
                a = jnp.where(mask, a, 0.0)
            out_a.append(accs[n] + jnp.dot(a.astype(BF16), v_ref[0, pl.ds(start, w), ls(c)],
                                           preferred_element_type=F32))
            out_c.append(carry + wts[n][:, w:])
        return out_c, out_a

    qs = []
    for pp in range(SB_PAIRS):
        for u in range(2):
            q2 = q_ref[0, u * tq:(u + 1) * tq, pp * LANES:(pp + 1) * LANES]
            q2 = (q2.astype(F32) * ATTN_SCALE).astype(BF16)
            for hh in range(2):
                hmask = (lane < HEAD_DIM) if hh == 0 else (lane >= HEAD_DIM)
                qs.append(jnp.where(hmask, q2, jnp.zeros_like(q2)))
    zero = jnp.zeros((tq, LANES), F32)
    base = pl.multiple_of(i * (2 * tq), 2 * tq)
    first = [c for c in range(nchain) if (c // 2) % 2 == 0]
    second = [c for c in range(nchain) if (c // 2) % 2 == 1]
    zeros = [zero] * len(first)
    c1, a1 = spans(first, base, tq, zeros, zeros, causal1)
    c2, a2 = spans(second, base, 2 * tq, zeros, zeros, causal2)
    carries, accs = [None] * nchain, [None] * nchain
    for n, c in enumerate(first):
        carries[c], accs[c] = c1[n], a1[n]
    for n, c in enumerate(second):
        carries[c], accs[c] = c2[n], a2[n]

    def alive(cs):
        m = cs[0]
        for x in cs[1:]:
            m = jnp.maximum(m, x)
        return jnp.max(m)

    def cond(st):
        r, _, _, m = st
        return jnp.logical_and(r < i, m > SB_DEAD_LOG)

    def body(st):
        r, cs, ac, _ = st
        start = pl.multiple_of((i - 1 - r) * (2 * tq), 2 * tq)
        cs, ac = spans(list(range(nchain)), start, 2 * tq, list(cs), list(ac), None)
        return r + 1, tuple(cs), tuple(ac), alive(cs)

    _, _, accs, _ = lax.while_loop(
        cond, body, (jnp.int32(0), tuple(carries), tuple(accs), alive(carries)))
    for pp in range(SB_PAIRS):
        for u in range(2):
            c = 4 * pp + 2 * u
            o_ref[0, u * tq:(u + 1) * tq, pp * LANES:(pp + 1) * LANES] = jnp.where(
                lane < HEAD_DIM, accs[c], accs[c + 1]).astype(o_ref.dtype)


def _sb_tri(w):
    r = np.arange(2 * w)[:, None] % w
    c = np.arange(w + LANES)[None, :]
    return jnp.asarray(np.where(c < w, r > c, True), dtype=BF16)


def sb_attention(qkv, batch, seq):
    ngroup = SB_HEADS * HEAD_DIM // (SB_PAIRS * LANES)
    width = SB_PAIRS * LANES
    return pl.pallas_call(
        _sb_kernel,
        grid=(batch, ngroup, seq // (2 * BLOCK)),
        in_specs=[pl.BlockSpec((1, 2 * BLOCK, width), lambda b, p, i: (b, i, p)),
                  pl.BlockSpec((1, seq, width), lambda b, p, i: (b, 0, ngroup + p)),
                  pl.BlockSpec((1, seq, width), lambda b, p, i: (b, 0, 2 * ngroup + p)),
                  pl.BlockSpec((2 * BLOCK, 2 * BLOCK), lambda b, p, i: (0, 0)),
                  pl.BlockSpec((4 * BLOCK, 3 * BLOCK), lambda b, p, i: (0, 0))],
        out_specs=pl.BlockSpec((1, 2 * BLOCK, width), lambda b, p, i: (b, i, p)),
        out_shape=jax.ShapeDtypeStruct((batch, seq, SB_HEADS * HEAD_DIM), BF16),
        compiler_params=_params("parallel", "parallel", "arbitrary"),
        name="sb_attention",
    )(qkv, qkv, qkv, _sb_tri(BLOCK), _sb_tri(2 * BLOCK))


def _alibi_slopes(n_heads):
    return [float(v) for v in np.asarray(
        2.0 ** (-8.0 * np.arange(1, n_heads + 1) / n_heads), dtype=np.float32)]


def _sw_kernel(sink_ref, q_ref, kp_ref, kc_ref, vp_ref, vc_ref, qg_ref, kg_ref, bd_ref, o_ref):
    n = pl.program_id(1)
    lane = lax.broadcasted_iota(jnp.int32, (BLOCK, LANES), 1)
    ii = lax.broadcasted_iota(jnp.int32, (BLOCK, 2 * BLOCK), 0)
    jj = lax.broadcasted_iota(jnp.int32, (BLOCK, 2 * BLOCK), 1)
    dist = BLOCK + ii - jj
    valid = (dist >= 0) & (dist < BLOCK) & jnp.logical_or(jj >= BLOCK, n > 0)
    distf = dist.astype(F32)
    slopes = _alibi_slopes(SW_Q_HEADS)
    bd = bd_ref[...]
    group = SW_Q_HEADS // SW_KV_HEADS
    heads = range(SW_Q_HEADS)
    kns, v2s = [], []
    for kk in range(SW_KV_HEADS):
        cs = slice(kk * LANES, (kk + 1) * LANES)
        k2 = jnp.concatenate([kp_ref[0, :, cs], kc_ref[0, :, cs]], axis=0)
        kns.append((k2 * _rms_scale(k2) * kg_ref[...]).astype(BF16))
        v2s.append(jnp.concatenate([vp_ref[0, :, cs], vc_ref[0, :, cs]], axis=0).astype(BF16))
    qns = []
    for pair in range(SW_Q_HEADS // 2):
        qq = q_ref[0, :, pair * LANES:(pair + 1) * LANES]
        sq = qq * qq
        hi = sq.astype(BF16)
        lo = (sq - hi.astype(F32)).astype(BF16)
        ssq = (jnp.dot(hi, bd, preferred_element_type=F32)
               + jnp.dot(lo, bd, preferred_element_type=F32))
        qn = qq * lax.rsqrt(ssq * (1.0 / HEAD_DIM) + RMS_EPS) * qg_ref[...]
        qns.append((qn * ATTN_SCALE).astype(BF16))
    ss = []
    for head in heads:
        qn = qns[head // 2]
        hmask = (lane < HEAD_DIM) if head % 2 == 0 else (lane >= HEAD_DIM)
        qh = jnp.where(hmask, qn, jnp.zeros_like(qn))
        s = lax.dot_general(qh, kns[head // group], _NT, preferred_element_type=F32)
        ss.append(jnp.where(valid, s - slopes[head] * distf, NEG_INF))
    ms = [jnp.maximum(jnp.max(ss[head], axis=-1, keepdims=True), sink_ref[head]) for head in heads]
    ps = [jnp.exp(ss[head] - ms[head]) for head in heads]
    denoms = [jnp.sum(ps[head], axis=-1, keepdims=True) + jnp.exp(sink_ref[head] - ms[head])
              for head in heads]
    outs = [jnp.dot(ps[head].astype(BF16), v2s[head // group], preferred_element_type=F32)
            / denoms[head] for head in heads]
    for pair in range(SW_Q_HEADS // 2):
        o_ref[0, :, pair * LANES:(pair + 1) * LANES] = jnp.where(
            lane < HEAD_DIM, outs[2 * pair], outs[2 * pair + 1]).astype(o_ref.dtype)


def sw_attention(qkv, sinks, q_gain, k_gain, batch, seq):
    nb = seq // BLOCK
    qw = SW_Q_HEADS * HEAD_DIM
    kw = SW_KV_HEADS * LANES
    bd = jnp.asarray(np.kron(np.eye(2), np.ones((HEAD_DIM, HEAD_DIM))), dtype=BF16)
    prev = lambda b, n: (b, jnp.maximum(n - 1, 0), qw // kw)
    cur = lambda b, n: (b, n, qw // kw)
    prev_v = lambda b, n: (b, jnp.maximum(n - 1, 0), qw // kw + 1)
    cur_v = lambda b, n: (b, n, qw // kw + 1)
    vec = lambda g: jnp.tile(g.astype(F32), 2).reshape(1, LANES)
    return pl.pallas_call(
        _sw_kernel,
        grid=(batch, nb),
        in_specs=[pl.BlockSpec(memory_space=pltpu.SMEM),
                  pl.BlockSpec((1, BLOCK, qw), lambda b, n: (b, n, 0)),
                  pl.BlockSpec((1, BLOCK, kw), prev),
                  pl.BlockSpec((1, BLOCK, kw), cur),
                  pl.BlockSpec((1, BLOCK, kw), prev_v),
                  pl.BlockSpec((1, BLOCK, kw), cur_v),
                  pl.BlockSpec((1, LANES), lambda b, n: (0, 0)),
                  pl.BlockSpec((1, LANES), lambda b, n: (0, 0)),
                  pl.BlockSpec((LANES, LANES), lambda b, n: (0, 0))],
        out_specs=pl.BlockSpec((1, BLOCK, qw), lambda b, n: (b, n, 0)),
        out_shape=jax.ShapeDtypeStruct((batch, seq, qw), BF16),
        compiler_params=_params("parallel", "arbitrary"),
        name="sw_attention",
    )(sinks.astype(F32), qkv, qkv, qkv, qkv, qkv, vec(q_gain), vec(k_gain), bd)


def _sw_qkv_weight(w):
    d = w.shape[0]
    qw = SW_Q_HEADS * HEAD_DIM
    kvw = SW_KV_HEADS * HEAD_DIM
    dup = lambda a: jnp.broadcast_to(
        a.reshape(d, SW_KV_HEADS, 1, HEAD_DIM), (d, SW_KV_HEADS, 2, HEAD_DIM)).reshape(d, 2 * kvw)
    return jnp.concatenate([w[:, :qw], dup(w[:, qw:qw + kvw]), dup(w[:, qw + kvw:])], axis=1)


def _proj_kernel(o_ref, wo_ref, h_ref, g_ref, wq_ref, hn_ref, xn_ref, q_ref):
    hn = h_ref[...] + jnp.dot(o_ref[...], wo_ref[...], preferred_element_type=F32)
    hn_ref[...] = hn
    xn = (hn * _rms_scale(hn) * g_ref[...]).astype(BF16)
    xn_ref[...] = xn
    q_ref[...] = jnp.dot(xn, wq_ref[...], preferred_element_type=F32).astype(BF16)


def proj_residual_query(o, wo, h, g, wq, tm=512):
    t, d = h.shape
    nq = wq.shape[1]
    tm = min(tm, t)
    row = lambda i: (i, 0)
    fixed = lambda i: (0, 0)
    return pl.pallas_call(
        _proj_kernel,
        grid=(t // tm,),
        in_specs=[pl.BlockSpec((tm, d), row), pl.BlockSpec((d, d), fixed),
                  pl.BlockSpec((tm, d), row), pl.BlockSpec((1, d), fixed),
                  pl.BlockSpec((d, nq), fixed)],
        out_specs=[pl.BlockSpec((tm, d), row), pl.BlockSpec((tm, d), row),
                   pl.BlockSpec((tm, nq), row)],
        out_shape=[jax.ShapeDtypeStruct((t, d), F32), jax.ShapeDtypeStruct((t, d), BF16),
                   jax.ShapeDtypeStruct((t, nq), BF16)],
        compiler_params=_params("parallel"),
        name="proj_residual_query",
    )(o, wo, h, g.reshape(1, d), wq)


def _cand_tables():
    k = PEER_TOPK
    flat = np.zeros((80,), np.float32)
    neg = np.zeros((80,), np.float32)
    for a in range(k):
        flat[a] = a * k
    for b in range(1, 8):
        for a in range(8):
            r = 16 + 8 * (b - 1) + a
            flat[r] = a * k + b
            if (a + 1) * (b + 1) > k:
                neg[r] = NEG_INF
    for b in range(8, k):
        flat[72 + b - 8] = b
    tile = lambda v: jnp.asarray(np.repeat(v[:, None], LANES, axis=1))
    return tile(flat), tile(neg)


def _dup_words(x):
    b = lax.bitcast_convert_type(x.astype(BF16).astype(F32), jnp.uint32)
    return b | (b >> 16)


def _pair_words(x, stage):
    half = x.shape[0] // 2
    stage[...] = x.astype(BF16).astype(F32)
    ev = lax.bitcast_convert_type(stage[pl.ds(0, half, stride=2), :], jnp.uint32)
    od = lax.bitcast_convert_type(stage[pl.ds(1, half, stride=2), :], jnp.uint32)
    return (ev >> 16) | (od & jnp.uint32(0xFFFF0000))


def _topk_kernel(q_ref, sk_ref, flat_ref, neg_ref, ns_ref, e1_ref, gt_ref,
                 s_scr, rk_scr, tv_scr, cand_scr, sel_scr, stage):
    tt = LANES
    nl = 2 * PEER_HEADS
    iota = lax.broadcasted_iota(jnp.int32, (PEER_KEYS, tt), 0).astype(F32)
    big = float(4 * PEER_KEYS * PEER_KEYS)
    unranked = float(PEER_TOPK)

    def scores(l):
        return lax.dot_general(sk_ref[l], q_ref[:, l * PEER_HALF:(l + 1) * PEER_HALF],
                               _NT, preferred_element_type=F32)

    def l1_reset(l):
        s_scr[l] = scores(l)
        rk_scr[l] = jnp.full((PEER_KEYS, tt), unranked, F32)

    def l1_round(l, r, exact):
        s = s_scr[l]
        m = jnp.max(s, axis=0, keepdims=True)
        hit = s == m
        if exact:
            hit = iota == jnp.min(jnp.where(hit, iota, big), axis=0, keepdims=True)
        s_scr[l] = jnp.where(hit, NEG_INF, s)
        rk_scr[l] = jnp.where(hit, jnp.asarray(r, F32), rk_scr[l])
        tv_scr[l, pl.ds(r, 1), :] = m

    def miscount(x):
        return jnp.abs(jnp.sum(x, axis=0, keepdims=True) - float(PEER_TOPK))

    def redo_where_miscounted(counts, redo):
        worst = counts[0]
        for c in counts[1:]:
            worst = jnp.maximum(worst, c)

        @pl.when(jnp.max(worst) > 0.0)
        def _():
            for idx, c in enumerate(counts):
                pl.when(jnp.max(c) > 0.0)(functools.partial(redo, idx))

    for l in range(nl):
        l1_reset(l)

    def l1_fast(r, c):
        for l in range(nl):
            l1_round(l, r, False)
        return c

    lax.fori_loop(0, PEER_TOPK, l1_fast, 0)

    def l1_redo(l):
        l1_reset(l)

        def l1_exact(r, c):
            l1_round(l, r, True)
            return c

        lax.fori_loop(0, PEER_TOPK, l1_exact, 0)

    redo_where_miscounted(
        [miscount(jnp.where(rk_scr[l] < unranked, 1.0, 0.0)) for l in range(nl)], l1_redo)

    flat = flat_ref[...]

    def l2_reset(h):
        tv1 = tv_scr[2 * h]
        tv2 = tv_scr[2 * h + 1]
        parts = [tv1 + tv2[0:1, :]]
        for b in range(1, 8):
            parts.append(tv1[0:8, :] + tv2[b:b + 1, :])
        parts.append(tv1[0:1, :] + tv2[8:16, :])
        cand_scr[h] = jnp.concatenate(parts, axis=0) + neg_ref[...]
        sel_scr[h] = jnp.zeros((80, tt), F32)

    def l2_round(h, exact):
        cd = cand_scr[h]
        m = jnp.max(cd, axis=0, keepdims=True)
        hit = cd == m
        if exact:
            hit = flat == jnp.min(jnp.where(hit, flat, big), axis=0, keepdims=True)
        cand_scr[h] = jnp.where(hit, NEG_INF, cd)
        sel_scr[h] = jnp.where(hit, 1.0, sel_scr[h])

    for h in range(PEER_HEADS):
        l2_reset(h)

    def l2_fast(r, c):
        for h in range(PEER_HEADS):
            l2_round(h, False)
        return c

    lax.fori_loop(0, PEER_TOPK, l2_fast, 0)

    def l2_redo(h):
        l2_reset(h)

        def l2_exact(r, c):
            l2_round(h, True)
            return c

        lax.fori_loop(0, PEER_TOPK, l2_exact, 0)

    redo_where_miscounted([miscount(sel_scr[h]) for h in range(PEER_HEADS)], l2_redo)

    row8 = lax.broadcasted_iota(jnp.int32, (8, tt), 0)
    for h in range(PEER_HEADS):
        tv1 = tv_scr[2 * h]
        tv2 = tv_scr[2 * h + 1]
        sel = sel_scr[h]
        x1 = jnp.exp(tv1 - tv1[0:1, :])
        x2 = jnp.exp(tv2 - tv2[0:1, :])
        eparts = [x1 * x2[0:1, :]]
        for b in range(1, 8):
            eparts.append(x1[0:8, :] * x2[b:b + 1, :])
        eparts.append(x1[0:1, :] * x2[8:16, :])
        z = jnp.sum(sel * jnp.concatenate(eparts, axis=0), axis=0, keepdims=True)
        n_lo = sel[0:8, :]
        for b in range(1, 8):
            n_lo = n_lo + sel[16 + 8 * (b - 1):24 + 8 * (b - 1), :]
        n_lo = n_lo + jnp.where(row8 == 0, jnp.sum(sel[72:80, :], axis=0, keepdims=True), 0.0)
        s1 = lax.dot_general(sk_ref[2 * h], q_ref[:, (2 * h) * PEER_HALF:(2 * h + 1) * PEER_HALF],
                             _NT, preferred_element_type=F32)
        s2 = lax.dot_general(sk_ref[2 * h + 1],
                             q_ref[:, (2 * h + 1) * PEER_HALF:(2 * h + 2) * PEER_HALF],
                             _NT, preferred_element_type=F32)
        rank1 = rk_scr[2 * h]
        r2 = rk_scr[2 * h + 1]
        a0 = jnp.sum(sel[0:PEER_TOPK, :], axis=0, keepdims=True)
        ns = jnp.where(rank1 < a0, 1.0, 0.0)
        for a in range(8):
            ns = jnp.where(rank1 == float(a), n_lo[a:a + 1, :], ns)
        ns_ref[h] = _dup_words(ns)
        e1_ref[h] = _dup_words(jnp.exp(s1 - tv1[0:1, :]) / z)
        for w, tab in enumerate((r2, jnp.exp(s2 - tv2[0:1, :]))):
            words = _pair_words(tab, stage)
            for k in range(PEER_KEYS // 16):
                gt_ref[_gt_row(k, h, w):_gt_row(k, h, w) + 8, :] = words[8 * k:8 * k + 8, :]


GT_ROWS = (PEER_KEYS // 16) * PEER_HEADS * 2 * 8


def _gt_row(k, h, w):
    return ((k * PEER_HEADS + h) * 2 + w) * 8


def peer_topk(q, sub_keys):
    t = q.shape[0]
    tt = LANES
    nl = 2 * PEER_HEADS
    flat, neg = _cand_tables()
    out_w = jax.ShapeDtypeStruct((PEER_HEADS, PEER_KEYS, t), jnp.uint32)
    out_g = jax.ShapeDtypeStruct((t // tt * GT_ROWS, tt), jnp.uint32)
    ospec = pl.BlockSpec((PEER_HEADS, PEER_KEYS, tt), lambda i: (0, 0, i))
    return pl.pallas_call(
        _topk_kernel,
        grid=(t // tt,),
        in_specs=[pl.BlockSpec((tt, nl * PEER_HALF), lambda i: (i, 0)),
                  pl.BlockSpec((nl, PEER_KEYS, PEER_HALF), lambda i: (0, 0, 0)),
                  pl.BlockSpec((80, tt), lambda i: (0, 0)),
                  pl.BlockSpec((80, tt), lambda i: (0, 0))],
        out_specs=[ospec, ospec, pl.BlockSpec((GT_ROWS, tt), lambda i: (i, 0))],
        out_shape=[out_w, out_w, out_g],
        scratch_shapes=[pltpu.VMEM((nl, PEER_KEYS, tt), F32),
                        pltpu.VMEM((nl, PEER_KEYS, tt), F32),
                        pltpu.VMEM((nl, PEER_TOPK, tt), F32),
                        pltpu.VMEM((PEER_HEADS, 80, tt), F32),
                        pltpu.VMEM((PEER_HEADS, 80, tt), F32),
                        pltpu.VMEM((PEER_KEYS, tt), F32)],
        compiler_params=_params("parallel"),
        name="peer_topk",
    )(q, sub_keys, flat, neg)


GELU_C = float(np.sqrt(2.0 / np.pi))


def _peer_kernel(xn_ref, u_ref, vt_ref, ns_ref, e1_ref, gt_ref, h_ref, o_ref,
                 acc_ref, hta_ref, htb_ref, cfa_ref, cfb_ref, *, et, tt):
    j = pl.program_id(1)
    rows = et // PEER_KEYS
    mb = 2 * LANES
    d = xn_ref.shape[1]

    @pl.when(j == 0)
    def _():
        acc_ref[...] = jnp.zeros_like(acc_ref)

    pk = 16
    zero = jnp.zeros((pk, LANES), BF16)

    def bcast(words, r):
        return pltpu.bitcast(jnp.broadcast_to(words[r:r + 1, :], (pk // 2, LANES)), BF16)

    pieces = [(rb, nb) for rb in range(et // mb) for nb in range(tt // mb)]

    def first_piece(tile, ht_ref, p):
        rb, nb = pieces[p]
        ht_ref[rb * mb:(rb + 1) * mb, nb * mb:(nb + 1) * mb] = lax.dot_general(
            u_ref[tile * et + rb * mb:tile * et + (rb + 1) * mb, :],
            xn_ref[nb * mb:(nb + 1) * mb, :], _NT, preferred_element_type=F32)

    def second_piece(tile, cf_ref, p):
        rb, nb = pieces[p]
        acc_ref[rb * mb:(rb + 1) * mb, nb * mb:(nb + 1) * mb] += jnp.dot(
            vt_ref[tile, rb * mb:(rb + 1) * mb, :], cf_ref[:, nb * mb:(nb + 1) * mb],
            preferred_element_type=F32)

    def gate_row(tile, ht_ref, cf_ref, r):
        base = pl.multiple_of((2 * j + tile) * rows, rows)
        for c in range(tt // LANES):
            cs = slice(c * LANES, (c + 1) * LANES)
            ns = [bcast(ns_ref[h, pl.ds(base, rows), cs], r) for h in range(PEER_HEADS)]
            e1 = [bcast(e1_ref[h, pl.ds(base, rows), cs], r) for h in range(PEER_HEADS)]
            for k in range(PEER_KEYS // pk):
                g = zero
                for h in range(PEER_HEADS):
                    r2 = pltpu.bitcast(gt_ref[pl.ds(c * GT_ROWS + _gt_row(k, h, 0), 8), :], BF16)
                    e2 = pltpu.bitcast(gt_ref[pl.ds(c * GT_ROWS + _gt_row(k, h, 1), 8), :], BF16)
                    g = g + jnp.where(r2 < ns[h], e2, zero) * e1[h]
                es = slice(r * PEER_KEYS + k * pk, r * PEER_KEYS + (k + 1) * pk)
                x = ht_ref[es, cs]
                t = jnp.tanh(x * (GELU_C + (GELU_C * 0.044715) * (x * x))).astype(BF16)
                hx = x.astype(BF16) * 0.5
                cf_ref[es, cs] = g * (hx + hx * t)

    npiece = len(pieces)
    assert npiece == rows and et == d
    for p in range(npiece):
        first_piece(0, hta_ref, p)
    for r in range(rows):
        gate_row(0, hta_ref, cfa_ref, r)
        first_piece(1, htb_ref, r)
    for r in range(rows):
        second_piece(0, cfa_ref, r)
        gate_row(1, htb_ref, cfb_ref, r)
    for p in range(npiece):
        second_piece(1, cfb_ref, p)

    @pl.when(j == pl.num_programs(1) - 1)
    def _():
        o_ref[...] = h_ref[...] + acc_ref[...].T


def peer_mix(xn, u, vt, ns, e1, gt, h, et=PEER_ET, tt=512):
    t, d = h.shape
    ne = u.shape[0]
    tt = min(tt, t)
    assert et % (8 * PEER_KEYS) == 0 and ne % (2 * et) == 0 and t % tt == 0
    once = pl.Buffered(1)
    gspec = pl.BlockSpec((PEER_HEADS, PEER_KEYS, tt), lambda i, j: (0, 0, i), pipeline_mode=once)
    return pl.pallas_call(
        functools.partial(_peer_kernel, et=et, tt=tt),
        grid=(t // tt, ne // (2 * et)),
        in_specs=[pl.BlockSpec((tt, d), lambda i, j: (i, 0), pipeline_mode=once),
                  pl.BlockSpec((2 * et, d), lambda i, j: (j, 0)),
                  pl.BlockSpec((2, d, et), lambda i, j: (j, 0, 0)),
                  gspec, gspec,
                  pl.BlockSpec((tt // LANES * GT_ROWS, LANES), lambda i, j: (i, 0), pipeline_mode=once),
                  pl.BlockSpec((tt, d), lambda i, j: (i, 0), pipeline_mode=once)],
        out_specs=pl.BlockSpec((tt, d), lambda i, j: (i, 0)),
        out_shape=jax.ShapeDtypeStruct((t, d), F32),
        scratch_shapes=[pltpu.VMEM((d, tt), F32),
                        pltpu.VMEM((et, tt), F32), pltpu.VMEM((et, tt), F32),
                        pltpu.VMEM((et, tt), BF16), pltpu.VMEM((et, tt), BF16)],
        compiler_params=_params("parallel", "arbitrary"),
        name="peer_mix",
    )(xn, u, vt, ns, e1, gt, h)


def _vt_tiles(v, et=PEER_ET):
    ne, d = v.shape
    return v.astype(BF16).reshape(ne // et, et, d).transpose(0, 2, 1)


def _peer_block(o, wo, h, g, wq, sub_keys, u, v):
    hn, xn, q = proj_residual_query(o, wo.astype(BF16), h, g, wq.astype(BF16))
    sk = sub_keys.reshape(2 * PEER_HEADS, PEER_KEYS, PEER_HALF).astype(BF16)
    ns, e1, gt = peer_topk(q, sk)
    return peer_mix(xn, u.astype(BF16), _vt_tiles(v), ns, e1, gt, hn)


def kernel(x, sb_norm, sb_w_qkv, sb_w_o, sw_norm, sw_w_qkv, sw_q_gain, sw_k_gain, sw_sinks, sw_w_o,
           ffn_norm, peer_w_query, peer_sub_keys, peer_u, peer_v):
    batch, seq, d = x.shape
    t = batch * seq
    h = x.reshape(t, d)

    qkv = norm_matmul(h, sb_norm[0], sb_w_qkv[0].astype(BF16), BF16)
    o = sb_attention(qkv.reshape(batch, seq, -1), batch, seq).reshape(t, -1)
    h = _peer_block(o, sb_w_o[0], h, ffn_norm[0], peer_w_query[0], peer_sub_keys[0],
                    peer_u[0], peer_v[0])

    qkv = norm_matmul(h, sw_norm[0], _sw_qkv_weight(sw_w_qkv[0]).astype(BF16), F32)
    o = sw_attention(qkv.reshape(batch, seq, -1), sw_sinks[0], sw_q_gain[0], sw_k_gain[0],
                     batch, seq).reshape(t, -1)
    h = _peer_block(o, sw_w_o[0], h, ffn_norm[1], peer_w_query[1], peer_sub_keys[1],
                    peer_u[1], peer_v[1])
    return h.reshape(batch, seq, d)
```

```python
import functools

import numpy as np
import jax
import jax.numpy as jnp
from jax import lax
from jax.experimental import pallas as pl
from jax.experimental.pallas import tpu as pltpu

F32 = jnp.float32
BF16 = jnp.bfloat16

D_MODEL = 1024
HEAD_DIM = 64
SB_HEADS = 16
SW_Q_HEADS = 16
SW_KV_HEADS = 4
BLOCK = 128
LANES = 128
PEER_HEADS = 8
PEER_KEYS = 128
PEER_HALF = 128
PEER_TOPK = 16
RMS_EPS = 1e-6
ATTN_SCALE = HEAD_DIM ** -0.5
SB_DEAD_LOG = -110.0
SB_PAIRS = 2
NEG_INF = float("-inf")
VMEM_LIMIT = 56 * 1024 * 1024
PEER_ET = 1024

_NT = (((1,), (1,)), ((), ()))


def _params(*sem, flags=None):
    return pltpu.CompilerParams(dimension_semantics=sem, vmem_limit_bytes=VMEM_LIMIT, flags=flags)


def _rms_scale(x):
    return lax.rsqrt(jnp.mean(x * x, axis=-1, keepdims=True) + RMS_EPS)


def _norm_matmul_kernel(x_ref, g_ref, w_ref, o_ref):
    x = x_ref[...]
    xn = (x * _rms_scale(x) * g_ref[...]).astype(BF16)
    o_ref[...] = jnp.dot(xn, w_ref[...], preferred_element_type=F32).astype(o_ref.dtype)


def norm_matmul(x, g, w, out_dtype, tm=512):
    t, d = x.shape
    n = w.shape[1]
    tm = min(tm, t)
    return pl.pallas_call(
        _norm_matmul_kernel,
        grid=(t // tm,),
        in_specs=[pl.BlockSpec((tm, d), lambda i: (i, 0)),
                  pl.BlockSpec((1, d), lambda i: (0, 0)),
                  pl.BlockSpec((d, n), lambda i: (0, 0))],
        out_specs=pl.BlockSpec((tm, n), lambda i: (i, 0)),
        out_shape=jax.ShapeDtypeStruct((t, n), out_dtype),
        compiler_params=_params("parallel"),
        name="norm_matmul",
    )(x, g.reshape(1, d), w)


def _sb_kernel(q_ref, k_ref, v_ref, tri1_ref, tri2_ref, o_ref):
    i = pl.program_id(2)
    tq = BLOCK
    nchain = 4 * SB_PAIRS
    lane = lax.broadcasted_iota(jnp.int32, (tq, LANES), 1)
    iota = lambda w, d: lax.broadcasted_iota(jnp.int32, (tq, w), d)
    causal1 = iota(tq, 1) < iota(tq, 0)
    causal2 = iota(2 * tq, 1) < iota(2 * tq, 0) + tq

    def spans(chains, start, w, carries, accs, mask):
        ls = lambda c: slice((c // 4) * LANES, (c // 4 + 1) * LANES)
        tri = tri1_ref[...] if w == tq else tri2_ref[...]
        zs = [lax.dot_general(qs[c], k_ref[0, pl.ds(start, w), ls(c)], _NT,
                              preferred_element_type=F32) for c in chains]
        lfs = [-(jnp.maximum(z, 0.0) + jnp.log1p(jnp.exp(-jnp.abs(z)))) for z in zs]
        wts, tots = [], []
        for lf in lfs:
            lfm = lf if mask is None else jnp.where(mask, lf, 0.0)
            hi = lfm.astype(BF16)
            lo = (lfm - hi.astype(F32)).astype(BF16)
            wt = jnp.dot(jnp.concatenate([hi, lo], axis=1), tri, preferred_element_type=F32)
            wts.append(wt)
            tots.append(jnp.broadcast_to(wt[:, 0:1] + lfm[:, 0:1], (tq, LANES)))
        out_c, out_a = [], []
        for n, c in enumerate(chains):
            carry = carries[n]
            cw = carry if w == tq else jnp.concatenate([carry, carry], axis=1)
            a = jnp.exp(zs[n] + lfs[n] + wts[n] + cw)
            if mask is not None:
                a = jnp.where(mask, a, 0.0)
            out_a.append(accs[n] + jnp.dot(a.astype(BF16), v_ref[0, pl.ds(start, w), ls(c)],
                                           preferred_element_type=F32))
            out_c.append(carry + tots[n])
        return out_c, out_a

    qs = []
    for pp in range(SB_PAIRS):
        for u in range(2):
            q2 = q_ref[0, u * tq:(u + 1) * tq, pp * LANES:(pp + 1) * LANES]
            q2 = (q2.astype(F32) * ATTN_SCALE).astype(BF16)
            for hh in range(2):
                hmask = (lane < HEAD_DIM) if hh == 0 else (lane >= HEAD_DIM)
                qs.append(jnp.where(hmask, q2, jnp.zeros_like(q2)))
    zero = jnp.zeros((tq, LANES), F32)
    base = pl.multiple_of(i * (2 * tq), 2 * tq)
    first = [c for c in range(nchain) if (c // 2) % 2 == 0]
    second = [c for c in range(nchain) if (c // 2) % 2 == 1]
    zeros = [zero] * len(first)
    c1, a1 = spans(first, base, tq, zeros, zeros, causal1)
    c2, a2 = spans(second, base, 2 * tq, zeros, zeros, causal2)
    carries, accs = [None] * nchain, [None] * nchain
    for n, c in enumerate(first):
        carries[c], accs[c] = c1[n], a1[n]
    for n, c in enumerate(second):
        carries[c], accs[c] = c2[n], a2[n]

    def alive(cs):
        m = cs[0]
        for x in cs[1:]:
            m = jnp.maximum(m, x)
        return jnp.max(m)

    def cond(st):
        r, _, _, m = st
        return jnp.logical_and(r < i, m > SB_DEAD_LOG)

    def body(st):
        r, cs, ac, _ = st
        start = pl.multiple_of((i - 1 - r) * (2 * tq), 2 * tq)
        cs, ac = spans(list(range(nchain)), start, 2 * tq, list(cs), list(ac), None)
        return r + 1, tuple(cs), tuple(ac), alive(cs)

    _, _, accs, _ = lax.while_loop(
        cond, body, (jnp.int32(0), tuple(carries), tuple(accs), alive(carries)))
    for pp in range(SB_PAIRS):
        for u in range(2):
            c = 4 * pp + 2 * u
            o_ref[0, u * tq:(u + 1) * tq, pp * LANES:(pp + 1) * LANES] = jnp.where(
                lane < HEAD_DIM, accs[c], accs[c + 1]).astype(o_ref.dtype)


def _sb_tri(w):
    r = np.arange(2 * w)[:, None] % w
    c = np.arange(w)[None, :]
    return jnp.asarray(r > c, dtype=BF16)


def sb_attention(qkv, batch, seq):
    ngroup = SB_HEADS * HEAD_DIM // (SB_PAIRS * LANES)
    width = SB_PAIRS * LANES
    return pl.pallas_call(
        _sb_kernel,
        grid=(batch, ngroup, seq // (2 * BLOCK)),
        in_specs=[pl.BlockSpec((1, 2 * BLOCK, width), lambda b, p, i: (b, i, p)),
                  pl.BlockSpec((1, seq, width), lambda b, p, i: (b, 0, ngroup + p)),
                  pl.BlockSpec((1, seq, width), lambda b, p, i: (b, 0, 2 * ngroup + p)),
                  pl.BlockSpec((2 * BLOCK, BLOCK), lambda b, p, i: (0, 0)),
                  pl.BlockSpec((4 * BLOCK, 2 * BLOCK), lambda b, p, i: (0, 0))],
        out_specs=pl.BlockSpec((1, 2 * BLOCK, width), lambda b, p, i: (b, i, p)),
        out_shape=jax.ShapeDtypeStruct((batch, seq, SB_HEADS * HEAD_DIM), BF16),
        compiler_params=_params("parallel", "parallel", "arbitrary"),
        name="sb_attention",
    )(qkv, qkv, qkv, _sb_tri(BLOCK), _sb_tri(2 * BLOCK))


def _alibi_slopes(n_heads):
    return [float(v) for v in np.asarray(
        2.0 ** (-8.0 * np.arange(1, n_heads + 1) / n_heads), dtype=np.float32)]


def _sw_kernel(sink_ref, q_ref, kp_ref, kc_ref, vp_ref, vc_ref, qg_ref, kg_ref, bd_ref, o_ref):
    n = pl.program_id(1)
    lane = lax.broadcasted_iota(jnp.int32, (BLOCK, LANES), 1)
    ii = lax.broadcasted_iota(jnp.int32, (BLOCK, 2 * BLOCK), 0)
    jj = lax.broadcasted_iota(jnp.int32, (BLOCK, 2 * BLOCK), 1)
    dist = BLOCK + ii - jj
    valid = (dist >= 0) & (dist < BLOCK) & jnp.logical_or(jj >= BLOCK, n > 0)
    distf = dist.astype(F32)
    slopes = _alibi_slopes(SW_Q_HEADS)
    bd = bd_ref[...]
    group = SW_Q_HEADS // SW_KV_HEADS
    heads = range(SW_Q_HEADS)
    kns, v2s = [], []
    for kk in range(SW_KV_HEADS):
        cs = slice(kk * LANES, (kk + 1) * LANES)
        k2 = jnp.concatenate([kp_ref[0, :, cs], kc_ref[0, :, cs]], axis=0)
        kns.append((k2 * _rms_scale(k2) * kg_ref[...]).astype(BF16))
        v2s.append(jnp.concatenate([vp_ref[0, :, cs], vc_ref[0, :, cs]], axis=0).astype(BF16))
    qns = []
    for pair in range(SW_Q_HEADS // 2):
        qq = q_ref[0, :, pair * LANES:(pair + 1) * LANES]
        sq = qq * qq
        hi = sq.astype(BF16)
        lo = (sq - hi.astype(F32)).astype(BF16)
        ssq = (jnp.dot(hi, bd, preferred_element_type=F32)
               + jnp.dot(lo, bd, preferred_element_type=F32))
        qn = qq * lax.rsqrt(ssq * (1.0 / HEAD_DIM) + RMS_EPS) * qg_ref[...]
        qns.append((qn * ATTN_SCALE).astype(BF16))
    ss = []
    for head in heads:
        qn = qns[head // 2]
        hmask = (lane < HEAD_DIM) if head % 2 == 0 else (lane >= HEAD_DIM)
        qh = jnp.where(hmask, qn, jnp.zeros_like(qn))
        s = lax.dot_general(qh, kns[head // group], _NT, preferred_element_type=F32)
        ss.append(jnp.where(valid, s - slopes[head] * distf, NEG_INF))
    ms = [jnp.maximum(jnp.max(ss[head], axis=-1, keepdims=True), sink_ref[head]) for head in heads]
    ps = [jnp.exp(ss[head] - ms[head]) for head in heads]
    denoms = [jnp.sum(ps[head], axis=-1, keepdims=True) + jnp.exp(sink_ref[head] - ms[head])
              for head in heads]
    outs = [jnp.dot(ps[head].astype(BF16), v2s[head // group], preferred_element_type=F32)
            / denoms[head] for head in heads]
    for pair in range(SW_Q_HEADS // 2):
        o_ref[0, :, pair * LANES:(pair + 1) * LANES] = jnp.where(
            lane < HEAD_DIM, outs[2 * pair], outs[2 * pair + 1]).astype(o_ref.dtype)


def sw_attention(qkv, sinks, q_gain, k_gain, batch, seq):
    nb = seq // BLOCK
    qw = SW_Q_HEADS * HEAD_DIM
    kw = SW_KV_HEADS * LANES
    bd = jnp.asarray(np.kron(np.eye(2), np.ones((HEAD_DIM, HEAD_DIM))), dtype=BF16)
    prev = lambda b, n: (b, jnp.maximum(n - 1, 0), qw // kw)
    cur = lambda b, n: (b, n, qw // kw)
    prev_v = lambda b, n: (b, jnp.maximum(n - 1, 0), qw // kw + 1)
    cur_v = lambda b, n: (b, n, qw // kw + 1)
    vec = lambda g: jnp.tile(g.astype(F32), 2).reshape(1, LANES)
    return pl.pallas_call(
        _sw_kernel,
        grid=(batch, nb),
        in_specs=[pl.BlockSpec(memory_space=pltpu.SMEM),
                  pl.BlockSpec((1, BLOCK, qw), lambda b, n: (b, n, 0)),
                  pl.BlockSpec((1, BLOCK, kw), prev),
                  pl.BlockSpec((1, BLOCK, kw), cur),
                  pl.BlockSpec((1, BLOCK, kw), prev_v),
                  pl.BlockSpec((1, BLOCK, kw), cur_v),
                  pl.BlockSpec((1, LANES), lambda b, n: (0, 0)),
                  pl.BlockSpec((1, LANES), lambda b, n: (0, 0)),
                  pl.BlockSpec((LANES, LANES), lambda b, n: (0, 0))],
        out_specs=pl.BlockSpec((1, BLOCK, qw), lambda b, n: (b, n, 0)),
        out_shape=jax.ShapeDtypeStruct((batch, seq, qw), BF16),
        compiler_params=_params("parallel", "arbitrary"),
        name="sw_attention",
    )(sinks.astype(F32), qkv, qkv, qkv, qkv, qkv, vec(q_gain), vec(k_gain), bd)


def _sw_qkv_weight(w):
    d = w.shape[0]
    qw = SW_Q_HEADS * HEAD_DIM
    kvw = SW_KV_HEADS * HEAD_DIM
    dup = lambda a: jnp.broadcast_to(
        a.reshape(d, SW_KV_HEADS, 1, HEAD_DIM), (d, SW_KV_HEADS, 2, HEAD_DIM)).reshape(d, 2 * kvw)
    return jnp.concatenate([w[:, :qw], dup(w[:, qw:qw + kvw]), dup(w[:, qw + kvw:])], axis=1)


def _proj_kernel(o_ref, wo_ref, h_ref, g_ref, wq_ref, hn_ref, xn_ref, q_ref):
    hn = h_ref[...] + jnp.dot(o_ref[...], wo_ref[...], preferred_element_type=F32)
    hn_ref[...] = hn
    xn = (hn * _rms_scale(hn) * g_ref[...]).astype(BF16)
    xn_ref[...] = xn
    q_ref[...] = jnp.dot(xn, wq_ref[...], preferred_element_type=F32).astype(BF16)


def proj_residual_query(o, wo, h, g, wq, tm=512):
    t, d = h.shape
    nq = wq.shape[1]
    tm = min(tm, t)
    row = lambda i: (i, 0)
    fixed = lambda i: (0, 0)
    return pl.pallas_call(
        _proj_kernel,
        grid=(t // tm,),
        in_specs=[pl.BlockSpec((tm, d), row), pl.BlockSpec((d, d), fixed),
                  pl.BlockSpec((tm, d), row), pl.BlockSpec((1, d), fixed),
                  pl.BlockSpec((d, nq), fixed)],
        out_specs=[pl.BlockSpec((tm, d), row), pl.BlockSpec((tm, d), row),
                   pl.BlockSpec((tm, nq), row)],
        out_shape=[jax.ShapeDtypeStruct((t, d), F32), jax.ShapeDtypeStruct((t, d), BF16),
                   jax.ShapeDtypeStruct((t, nq), BF16)],
        compiler_params=_params("parallel"),
        name="proj_residual_query",
    )(o, wo, h, g.reshape(1, d), wq)


def _cand_tables():
    k = PEER_TOPK
    flat = np.zeros((80,), np.float32)
    neg = np.zeros((80,), np.float32)
    for a in range(k):
        flat[a] = a * k
    for b in range(1, 8):
        for a in range(8):
            r = 16 + 8 * (b - 1) + a
            flat[r] = a * k + b
            if (a + 1) * (b + 1) > k:
                neg[r] = NEG_INF
    for b in range(8, k):
        flat[72 + b - 8] = b
    tile = lambda v: jnp.asarray(np.repeat(v[:, None], LANES, axis=1))
    return tile(flat), tile(neg)


def _dup_words(x):
    b = lax.bitcast_convert_type(x.astype(BF16).astype(F32), jnp.uint32)
    return b | (b >> 16)


def _pair_words(x, stage):
    half = x.shape[0] // 2
    stage[...] = x.astype(BF16).astype(F32)
    ev = lax.bitcast_convert_type(stage[pl.ds(0, half, stride=2), :], jnp.uint32)
    od = lax.bitcast_convert_type(stage[pl.ds(1, half, stride=2), :], jnp.uint32)
    return (ev >> 16) | (od & jnp.uint32(0xFFFF0000))


def _topk_kernel(q_ref, sk_ref, flat_ref, neg_ref, ns_ref, e1_ref, gt_ref,
                 s_scr, rk_scr, tv_scr, cand_scr, sel_scr, stage):
    tt = LANES
    nl = 2 * PEER_HEADS
    iota = lax.broadcasted_iota(jnp.int32, (PEER_KEYS, tt), 0).astype(F32)
    big = float(4 * PEER_KEYS * PEER_KEYS)
    unranked = float(PEER_TOPK)

    def scores(l):
        return lax.dot_general(sk_ref[l], q_ref[:, l * PEER_HALF:(l + 1) * PEER_HALF],
                               _NT, preferred_element_type=F32)

    def l1_reset(l):
        s_scr[l] = scores(l)
        rk_scr[l] = jnp.full((PEER_KEYS, tt), unranked, F32)

    def l1_round(l, r, exact, unroll=1):
        s = s_scr[l]
        rk = rk_scr[l]
        for n in range(unroll):
            rr = unroll * r + n
            m = jnp.max(s, axis=0, keepdims=True)
            hit = s == m
            if exact:
                hit = iota == jnp.min(jnp.where(hit, iota, big), axis=0, keepdims=True)
            s = jnp.where(hit, NEG_INF, s)
            rk = jnp.where(hit, jnp.asarray(rr, F32), rk)
            tv_scr[l, pl.ds(rr, 1), :] = m
        s_scr[l] = s
        rk_scr[l] = rk

    def miscount(x):
        return jnp.abs(jnp.sum(x, axis=0, keepdims=True) - float(PEER_TOPK))

    def redo_where_miscounted(counts, redo):
        worst = counts[0]
        for c in counts[1:]:
            worst = jnp.maximum(worst, c)

        @pl.when(jnp.max(worst) > 0.0)
        def _():
            for idx, c in enumerate(counts):
                pl.when(jnp.max(c) > 0.0)(functools.partial(redo, idx))

    for l in range(nl):
        l1_reset(l)

    def l1_fast(r, c):
        for l in range(nl):
            l1_round(l, r, False, unroll=L1_UNROLL)
        return c

    lax.fori_loop(0, PEER_TOPK // L1_UNROLL, l1_fast, 0)

    def l1_redo(l):
        l1_reset(l)

        def l1_exact(r, c):
            l1_round(l, r, True)
            return c

        lax.fori_loop(0, PEER_TOPK, l1_exact, 0)

    redo_where_miscounted(
        [miscount(jnp.where(rk_scr[l] < unranked, 1.0, 0.0)) for l in range(nl)], l1_redo)

    flat = flat_ref[...]

    def l2_reset(h):
        tv1 = tv_scr[2 * h]
        tv2 = tv_scr[2 * h + 1]
        parts = [tv1 + tv2[0:1, :]]
        for b in range(1, 8):
            parts.append(tv1[0:8, :] + tv2[b:b + 1, :])
        parts.append(tv1[0:1, :] + tv2[8:16, :])
        cand_scr[h] = jnp.concatenate(parts, axis=0) + neg_ref[...]
        sel_scr[h] = jnp.zeros((80, tt), F32)

    def l2_round(h, exact, unroll=1):
        cd = cand_scr[h]
        sel = sel_scr[h]
        for _ in range(unroll):
            m = jnp.max(cd, axis=0, keepdims=True)
            hit = cd == m
            if exact:
                hit = flat == jnp.min(jnp.where(hit, flat, big), axis=0, keepdims=True)
            cd = jnp.where(hit, NEG_INF, cd)
            sel = jnp.where(hit, 1.0, sel)
        cand_scr[h] = cd
        sel_scr[h] = sel

    for h in range(PEER_HEADS):
        l2_reset(h)

    def l2_fast(r, c):
        for h in range(PEER_HEADS):
            l2_round(h, False, unroll=L1_UNROLL)
        return c

    lax.fori_loop(0, PEER_TOPK // L1_UNROLL, l2_fast, 0)

    def l2_redo(h):
        l2_reset(h)

        def l2_exact(r, c):
            l2_round(h, True)
            return c

        lax.fori_loop(0, PEER_TOPK, l2_exact, 0)

    redo_where_miscounted([miscount(sel_scr[h]) for h in range(PEER_HEADS)], l2_redo)

    row8 = lax.broadcasted_iota(jnp.int32, (8, tt), 0)
    for h in range(PEER_HEADS):
        tv1 = tv_scr[2 * h]
        tv2 = tv_scr[2 * h + 1]
        sel = sel_scr[h]
        x1 = jnp.exp(tv1 - tv1[0:1, :])
        x2 = jnp.exp(tv2 - tv2[0:1, :])
        eparts = [x1 * x2[0:1, :]]
        for b in range(1, 8):
            eparts.append(x1[0:8, :] * x2[b:b + 1, :])
        eparts.append(x1[0:1, :] * x2[8:16, :])
        z = jnp.sum(sel * jnp.concatenate(eparts, axis=0), axis=0, keepdims=True)
        n_lo = sel[0:8, :]
        for b in range(1, 8):
            n_lo = n_lo + sel[16 + 8 * (b - 1):24 + 8 * (b - 1), :]
        n_lo = n_lo + jnp.where(row8 == 0, jnp.sum(sel[72:80, :], axis=0, keepdims=True), 0.0)
        s1 = lax.dot_general(sk_ref[2 * h], q_ref[:, (2 * h) * PEER_HALF:(2 * h + 1) * PEER_HALF],
                             _NT, preferred_element_type=F32)
        s2 = lax.dot_general(sk_ref[2 * h + 1],
                             q_ref[:, (2 * h + 1) * PEER_HALF:(2 * h + 2) * PEER_HALF],
                             _NT, preferred_element_type=F32)
        rank1 = rk_scr[2 * h]
        r2 = rk_scr[2 * h + 1]
        a0 = jnp.sum(sel[0:PEER_TOPK, :], axis=0, keepdims=True)
        ns = jnp.where(rank1 < a0, 1.0, 0.0)
        for a in range(8):
            ns = jnp.where(rank1 == float(a), n_lo[a:a + 1, :], ns)
        ns_ref[h] = _dup_words(ns)
        e1_ref[h] = _dup_words(jnp.exp(s1 - tv1[0:1, :]) / z)
        for w, tab in enumerate((r2, jnp.exp(s2 - tv2[0:1, :]))):
            words = _pair_words(tab, stage)
            for k in range(PEER_KEYS // 16):
                gt_ref[_gt_row(k, h, w):_gt_row(k, h, w) + 8, :] = words[8 * k:8 * k + 8, :]


GT_ROWS = (PEER_KEYS // 16) * PEER_HEADS * 2 * 8


def _gt_row(k, h, w):
    return ((k * PEER_HEADS + h) * 2 + w) * 8


def peer_topk(q, sub_keys):
    t = q.shape[0]
    tt = LANES
    nl = 2 * PEER_HEADS
    flat, neg = _cand_tables()
    out_w = jax.ShapeDtypeStruct((PEER_HEADS, PEER_KEYS, t), jnp.uint32)
    out_g = jax.ShapeDtypeStruct((t // tt * GT_ROWS, tt), jnp.uint32)
    ospec = pl.BlockSpec((PEER_HEADS, PEER_KEYS, tt), lambda i: (0, 0, i))
    return pl.pallas_call(
        _topk_kernel,
        grid=(t // tt,),
        in_specs=[pl.BlockSpec((tt, nl * PEER_HALF), lambda i: (i, 0)),
                  pl.BlockSpec((nl, PEER_KEYS, PEER_HALF), lambda i: (0, 0, 0)),
                  pl.BlockSpec((80, tt), lambda i: (0, 0)),
                  pl.BlockSpec((80, tt), lambda i: (0, 0))],
        out_specs=[ospec, ospec, pl.BlockSpec((GT_ROWS, tt), lambda i: (i, 0))],
        out_shape=[out_w, out_w, out_g],
        scratch_shapes=[pltpu.VMEM((nl, PEER_KEYS, tt), F32),
                        pltpu.VMEM((nl, PEER_KEYS, tt), F32),
                        pltpu.VMEM((nl, PEER_TOPK, tt), F32),
                        pltpu.VMEM((PEER_HEADS, 80, tt), F32),
                        pltpu.VMEM((PEER_HEADS, 80, tt), F32),
                        pltpu.VMEM((PEER_KEYS, tt), F32)],
        compiler_params=_params("parallel"),
        name="peer_topk",
    )(q, sub_keys, flat, neg)


GELU_C = float(np.sqrt(2.0 / np.pi))
L1_UNROLL = 2


def _peer_kernel(xn_ref, u_ref, vt_ref, ns_ref, e1_ref, gt_ref, h_ref, o_ref,
                 acc_ref, hta_ref, htb_ref, cfa_ref, cfb_ref, *, et, tt):
    j = pl.program_id(1)
    rows = et // PEER_KEYS
    mb = 2 * LANES
    d = xn_ref.shape[1]

    @pl.when(j == 0)
    def _():
        acc_ref[...] = jnp.zeros_like(acc_ref)

    pk = 16
    zero = jnp.zeros((pk, LANES), BF16)

    def bcast(words, r):
        return pltpu.bitcast(jnp.broadcast_to(words[r:r + 1, :], (pk // 2, LANES)), BF16)

    pieces = [(rb, nb) for rb in range(et // mb) for nb in range(tt // mb)]

    def first_piece(tile, ht_ref, p):
        rb, nb = pieces[p]
        ht_ref[rb * mb:(rb + 1) * mb, nb * mb:(nb + 1) * mb] = lax.dot_general(
            u_ref[tile * et + rb * mb:tile * et + (rb + 1) * mb, :],
            xn_ref[nb * mb:(nb + 1) * mb, :], _NT, preferred_element_type=F32)

    def second_piece(tile, cf_ref, p):
        rb, nb = pieces[p]
        acc_ref[rb * mb:(rb + 1) * mb, nb * mb:(nb + 1) * mb] += jnp.dot(
            vt_ref[tile, rb * mb:(rb + 1) * mb, :], cf_ref[:, nb * mb:(nb + 1) * mb],
            preferred_element_type=F32)

    def gate_row(tile, ht_ref, cf_ref, r):
        base = pl.multiple_of((2 * j + tile) * rows, rows)
        for c in range(tt // LANES):
            cs = slice(c * LANES, (c + 1) * LANES)
            ns = [bcast(ns_ref[h, pl.ds(base, rows), cs], r) for h in range(PEER_HEADS)]
            e1 = [bcast(e1_ref[h, pl.ds(base, rows), cs], r) for h in range(PEER_HEADS)]
            for k in range(PEER_KEYS // pk):
                g = zero
                for h in range(PEER_HEADS):
                    r2 = pltpu.bitcast(gt_ref[pl.ds(c * GT_ROWS + _gt_row(k, h, 0), 8), :], BF16)
                    e2 = pltpu.bitcast(gt_ref[pl.ds(c * GT_ROWS + _gt_row(k, h, 1), 8), :], BF16)
                    g = g + jnp.where(r2 < ns[h], e2, zero) * e1[h]
                es = slice(r * PEER_KEYS + k * pk, r * PEER_KEYS + (k + 1) * pk)
                x = ht_ref[es, cs]
                t = jnp.tanh(x * (GELU_C + (GELU_C * 0.044715) * (x * x))).astype(BF16)
                hx = x.astype(BF16) * 0.5
                cf_ref[es, cs] = g * (hx + hx * t)

    npiece = len(pieces)
    assert npiece == rows and et == d
    for p in range(npiece):
        first_piece(0, hta_ref, p)
    for r in range(rows):
        gate_row(0, hta_ref, cfa_ref, r)
        first_piece(1, htb_ref, r)
    for r in range(rows):
        second_piece(0, cfa_ref, r)
        gate_row(1, htb_ref, cfb_ref, r)
    for p in range(npiece):
        second_piece(1, cfb_ref, p)

    @pl.when(j == pl.num_programs(1) - 1)
    def _():
        o_ref[...] = h_ref[...] + acc_ref[...].T


def peer_mix(xn, u, vt, ns, e1, gt, h, et=PEER_ET, tt=512):
    t, d = h.shape
    ne = u.shape[0]
    tt = min(tt, t)
    assert et % (8 * PEER_KEYS) == 0 and ne % (2 * et) == 0 and t % tt == 0
    once = pl.Buffered(1)
    gspec = pl.BlockSpec((PEER_HEADS, PEER_KEYS, tt), lambda i, j: (0, 0, i), pipeline_mode=once)
    return pl.pallas_call(
        functools.partial(_peer_kernel, et=et, tt=tt),
        grid=(t // tt, ne // (2 * et)),
        in_specs=[pl.BlockSpec((tt, d), lambda i, j: (i, 0), pipeline_mode=once),
                  pl.BlockSpec((2 * et, d), lambda i, j: (j, 0)),
                  pl.BlockSpec((2, d, et), lambda i, j: (j, 0, 0)),
                  gspec, gspec,
                  pl.BlockSpec((tt // LANES * GT_ROWS, LANES), lambda i, j: (i, 0), pipeline_mode=once),
                  pl.BlockSpec((tt, d), lambda i, j: (i, 0), pipeline_mode=once)],
        out_specs=pl.BlockSpec((tt, d), lambda i, j: (i, 0)),
        out_shape=jax.ShapeDtypeStruct((t, d), F32),
        scratch_shapes=[pltpu.VMEM((d, tt), F32),
                        pltpu.VMEM((et, tt), F32), pltpu.VMEM((et, tt), F32),
                        pltpu.VMEM((et, tt), BF16), pltpu.VMEM((et, tt), BF16)],
        compiler_params=_params("parallel", "arbitrary"),
        name="peer_mix",
    )(xn, u, vt, ns, e1, gt, h)


def _vt_tiles(v, et=PEER_ET):
    ne, d = v.shape
    return v.astype(BF16).reshape(ne // et, et, d).transpose(0, 2, 1)


def _peer_block(o, wo, h, g, wq, sub_keys, u, v):
    hn, xn, q = proj_residual_query(o, wo.astype(BF16), h, g, wq.astype(BF16))
    sk = sub_keys.reshape(2 * PEER_HEADS, PEER_KEYS, PEER_HALF).astype(BF16)
    ns, e1, gt = peer_topk(q, sk)
    return peer_mix(xn, u.astype(BF16), _vt_tiles(v), ns, e1, gt, hn)


def kernel(x, sb_norm, sb_w_qkv, sb_w_o, sw_norm, sw_w_qkv, sw_q_gain, sw_k_gain, sw_sinks, sw_w_o,
           ffn_norm, peer_w_query, peer_sub_keys, peer_u, peer_v):
    batch, seq, d = x.shape
    t = batch * seq
    h = x.reshape(t, d)

    qkv = norm_matmul(h, sb_norm[0], sb_w_qkv[0].astype(BF16), BF16)
    o = sb_attention(qkv.reshape(batch, seq, -1), batch, seq).reshape(t, -1)
    h = _peer_block(o, sb_w_o[0], h, ffn_norm[0], peer_w_query[0], peer_sub_keys[0],
                    peer_u[0], peer_v[0])

    qkv = norm_matmul(h, sw_norm[0], _sw_qkv_weight(sw_w_qkv[0]).astype(BF16), F32)
    o = sw_attention(qkv.reshape(batch, seq, -1), sw_sinks[0], sw_q_gain[0], sw_k_gain[0],
                     batch, seq).reshape(t, -1)
    h = _peer_block(o, sw_w_o[0], h, ffn_norm[1], peer_w_query[1], peer_sub_keys[1],
                    peer_u[1], peer_v[1])
    return h.reshape(batch, seq, d)
```

```python
import functools

import numpy as np
import jax
import jax.numpy as jnp
from jax import lax
from jax.experimental import pallas as pl
from jax.experimental.pallas import tpu as pltpu

F32 = jnp.float32
BF16 = jnp.bfloat16

D_MODEL = 1024
HEAD_DIM = 64
SB_HEADS = 16
SW_Q_HEADS = 16
SW_KV_HEADS = 4
BLOCK = 128
LANES = 128
PEER_HEADS = 8
PEER_KEYS = 128
PEER_HALF = 128
PEER_TOPK = 16
RMS_EPS = 1e-6
ATTN_SCALE = HEAD_DIM ** -0.5
SB_DEAD_LOG = -110.0
SB_PAIRS = 4
NEG_INF = float("-inf")
VMEM_LIMIT = 56 * 1024 * 1024
PEER_ET = 1024

_NT = (((1,), (1,)), ((), ()))


def _params(*sem, flags=None):
    return pltpu.CompilerParams(dimension_semantics=sem, vmem_limit_bytes=VMEM_LIMIT, flags=flags)


def _rms_scale(x):
    return lax.rsqrt(jnp.mean(x * x, axis=-1, keepdims=True) + RMS_EPS)


def _norm_matmul_kernel(x_ref, g_ref, w_ref, o_ref):
    x = x_ref[...]
    xn = (x * _rms_scale(x) * g_ref[...]).astype(BF16)
    o_ref[...] = jnp.dot(xn, w_ref[...], preferred_element_type=F32).astype(o_ref.dtype)


def norm_matmul(x, g, w, out_dtype, tm=512):
    t, d = x.shape
    n = w.shape[1]
    tm = min(tm, t)
    return pl.pallas_call(
        _norm_matmul_kernel,
        grid=(t // tm,),
        in_specs=[pl.BlockSpec((tm, d), lambda i: (i, 0)),
                  pl.BlockSpec((1, d), lambda i: (0, 0)),
                  pl.BlockSpec((d, n), lambda i: (0, 0))],
        out_specs=pl.BlockSpec((tm, n), lambda i: (i, 0)),
        out_shape=jax.ShapeDtypeStruct((t, n), out_dtype),
        compiler_params=_params("parallel"),
        name="norm_matmul",
    )(x, g.reshape(1, d), w)


def _sb_kernel(q_ref, k_ref, v_ref, tri1_ref, tri2_ref, o_ref):
    i = pl.program_id(2)
    tq = BLOCK
    nchain = 4 * SB_PAIRS
    lane = lax.broadcasted_iota(jnp.int32, (tq, LANES), 1)
    iota = lambda w, d: lax.broadcasted_iota(jnp.int32, (tq, w), d)
    causal1 = iota(tq, 1) < iota(tq, 0)
    causal2 = iota(2 * tq, 1) < iota(2 * tq, 0) + tq

    def spans(chains, start, w, carries, accs, mask):
        ls = lambda c: slice((c // 4) * LANES, (c // 4 + 1) * LANES)
        tri = tri1_ref[...] if w == tq else tri2_ref[...]
        zs = [lax.dot_general(qs[c], k_ref[0, pl.ds(start, w), ls(c)], _NT,
                              preferred_element_type=F32) for c in chains]
        lfs = [-(jnp.maximum(z, 0.0) + jnp.log1p(jnp.exp(-jnp.abs(z)))) for z in zs]
        wts, tots = [], []
        for lf in lfs:
            lfm = lf if mask is None else jnp.where(mask, lf, 0.0)
            hi = lfm.astype(BF16)
            lo = (lfm - hi.astype(F32)).astype(BF16)
            wt = jnp.dot(jnp.concatenate([hi, lo], axis=1), tri, preferred_element_type=F32)
            wts.append(wt)
            tots.append(jnp.broadcast_to(wt[:, 0:1] + lfm[:, 0:1], (tq, LANES)))
        out_c, out_a = [], []
        for n, c in enumerate(chains):
            carry = carries[n]
            cw = carry if w == tq else jnp.concatenate([carry, carry], axis=1)
            a = jnp.exp(zs[n] + lfs[n] + wts[n] + cw)
            if mask is not None:
                a = jnp.where(mask, a, 0.0)
            out_a.append(accs[n] + jnp.dot(a.astype(BF16), v_ref[0, pl.ds(start, w), ls(c)],
                                           preferred_element_type=F32))
            out_c.append(carry + tots[n])
        return out_c, out_a

    qs = []
    for pp in range(SB_PAIRS):
        for u in range(2):
            q2 = q_ref[0, u * tq:(u + 1) * tq, pp * LANES:(pp + 1) * LANES]
            q2 = (q2.astype(F32) * ATTN_SCALE).astype(BF16)
            for hh in range(2):
                hmask = (lane < HEAD_DIM) if hh == 0 else (lane >= HEAD_DIM)
                qs.append(jnp.where(hmask, q2, jnp.zeros_like(q2)))
    zero = jnp.zeros((tq, LANES), F32)
    base = pl.multiple_of(i * (2 * tq), 2 * tq)
    first = [c for c in range(nchain) if (c // 2) % 2 == 0]
    second = [c for c in range(nchain) if (c // 2) % 2 == 1]
    zeros = [zero] * len(first)
    c1, a1 = spans(first, base, tq, zeros, zeros, causal1)
    c2, a2 = spans(second, base, 2 * tq, zeros, zeros, causal2)
    carries, accs = [None] * nchain, [None] * nchain
    for n, c in enumerate(first):
        carries[c], accs[c] = c1[n], a1[n]
    for n, c in enumerate(second):
        carries[c], accs[c] = c2[n], a2[n]

    def alive(cs):
        m = cs[0]
        for x in cs[1:]:
            m = jnp.maximum(m, x)
        return jnp.max(m)

    def cond(st):
        r, _, _, m = st
        return jnp.logical_and(r < i, m > SB_DEAD_LOG)

    def body(st):
        r, cs, ac, _ = st
        start = pl.multiple_of((i - 1 - r) * (2 * tq), 2 * tq)
        cs, ac = spans(list(range(nchain)), start, 2 * tq, list(cs), list(ac), None)
        return r + 1, tuple(cs), tuple(ac), alive(cs)

    _, _, accs, _ = lax.while_loop(
        cond, body, (jnp.int32(0), tuple(carries), tuple(accs), alive(carries)))
    for pp in range(SB_PAIRS):
        for u in range(2):
            c = 4 * pp + 2 * u
            o_ref[0, u * tq:(u + 1) * tq, pp * LANES:(pp + 1) * LANES] = jnp.where(
                lane < HEAD_DIM, accs[c], accs[c + 1]).astype(o_ref.dtype)


def _sb_tri(w):
    r = np.arange(2 * w)[:, None] % w
    c = np.arange(w)[None, :]
    return jnp.asarray(r > c, dtype=BF16)


def sb_attention(qkv, batch, seq):
    ngroup = SB_HEADS * HEAD_DIM // (SB_PAIRS * LANES)
    width = SB_PAIRS * LANES
    return pl.pallas_call(
        _sb_kernel,
        grid=(batch, ngroup, seq // (2 * BLOCK)),
        in_specs=[pl.BlockSpec((1, 2 * BLOCK, width), lambda b, p, i: (b, i, p)),
                  pl.BlockSpec((1, seq, width), lambda b, p, i: (b, 0, ngroup + p)),
                  pl.BlockSpec((1, seq, width), lambda b, p, i: (b, 0, 2 * ngroup + p)),
                  pl.BlockSpec((2 * BLOCK, BLOCK), lambda b, p, i: (0, 0)),
                  pl.BlockSpec((4 * BLOCK, 2 * BLOCK), lambda b, p, i: (0, 0))],
        out_specs=pl.BlockSpec((1, 2 * BLOCK, width), lambda b, p, i: (b, i, p)),
        out_shape=jax.ShapeDtypeStruct((batch, seq, SB_HEADS * HEAD_DIM), BF16),
        compiler_params=_params("parallel", "parallel", "arbitrary"),
        name="sb_attention",
    )(qkv, qkv, qkv, _sb_tri(BLOCK), _sb_tri(2 * BLOCK))


def _alibi_slopes(n_heads):
    return [float(v) for v in np.asarray(
        2.0 ** (-8.0 * np.arange(1, n_heads + 1) / n_heads), dtype=np.float32)]


def _sw_kernel(sink_ref, q_ref, kp_ref, kc_ref, vp_ref, vc_ref, qg_ref, kg_ref, bd_ref, o_ref):
    n = pl.program_id(1)
    lane = lax.broadcasted_iota(jnp.int32, (BLOCK, LANES), 1)
    ii = lax.broadcasted_iota(jnp.int32, (BLOCK, 2 * BLOCK), 0)
    jj = lax.broadcasted_iota(jnp.int32, (BLOCK, 2 * BLOCK), 1)
    dist = BLOCK + ii - jj
    valid = (dist >= 0) & (dist < BLOCK) & jnp.logical_or(jj >= BLOCK, n > 0)
    distf = dist.astype(F32)
    slopes = _alibi_slopes(SW_Q_HEADS)
    bd = bd_ref[...]
    group = SW_Q_HEADS // SW_KV_HEADS
    heads = range(SW_Q_HEADS)
    kns, v2s = [], []
    for kk in range(SW_KV_HEADS):
        cs = slice(kk * LANES, (kk + 1) * LANES)
        k2 = jnp.concatenate([kp_ref[0, :, cs], kc_ref[0, :, cs]], axis=0)
        kns.append((k2 * _rms_scale(k2) * kg_ref[...]).astype(BF16))
        v2s.append(jnp.concatenate([vp_ref[0, :, cs], vc_ref[0, :, cs]], axis=0).astype(BF16))
    qns = []
    for pair in range(SW_Q_HEADS // 2):
        qq = q_ref[0, :, pair * LANES:(pair + 1) * LANES]
        sq = qq * qq
        hi = sq.astype(BF16)
        lo = (sq - hi.astype(F32)).astype(BF16)
        ssq = (jnp.dot(hi, bd, preferred_element_type=F32)
               + jnp.dot(lo, bd, preferred_element_type=F32))
        qn = qq * lax.rsqrt(ssq * (1.0 / HEAD_DIM) + RMS_EPS) * qg_ref[...]
        qns.append((qn * ATTN_SCALE).astype(BF16))
    ss = []
    for head in heads:
        qn = qns[head // 2]
        hmask = (lane < HEAD_DIM) if head % 2 == 0 else (lane >= HEAD_DIM)
        qh = jnp.where(hmask, qn, jnp.zeros_like(qn))
        s = lax.dot_general(qh, kns[head // group], _NT, preferred_element_type=F32)
        ss.append(jnp.where(valid, s - slopes[head] * distf, NEG_INF))
    ms = [jnp.maximum(jnp.max(ss[head], axis=-1, keepdims=True), sink_ref[head]) for head in heads]
    ps = [jnp.exp(ss[head] - ms[head]) for head in heads]
    denoms = [jnp.sum(ps[head], axis=-1, keepdims=True) + jnp.exp(sink_ref[head] - ms[head])
              for head in heads]
    outs = [jnp.dot(ps[head].astype(BF16), v2s[head // group], preferred_element_type=F32)
            / denoms[head] for head in heads]
    for pair in range(SW_Q_HEADS // 2):
        o_ref[0, :, pair * LANES:(pair + 1) * LANES] = jnp.where(
            lane < HEAD_DIM, outs[2 * pair], outs[2 * pair + 1]).astype(o_ref.dtype)


def sw_attention(qkv, sinks, q_gain, k_gain, batch, seq):
    nb = seq // BLOCK
    qw = SW_Q_HEADS * HEAD_DIM
    kw = SW_KV_HEADS * LANES
    bd = jnp.asarray(np.kron(np.eye(2), np.ones((HEAD_DIM, HEAD_DIM))), dtype=BF16)
    prev = lambda b, n: (b, jnp.maximum(n - 1, 0), qw // kw)
    cur = lambda b, n: (b, n, qw // kw)
    prev_v = lambda b, n: (b, jnp.maximum(n - 1, 0), qw // kw + 1)
    cur_v = lambda b, n: (b, n, qw // kw + 1)
    vec = lambda g: jnp.tile(g.astype(F32), 2).reshape(1, LANES)
    return pl.pallas_call(
        _sw_kernel,
        grid=(batch, nb),
        in_specs=[pl.BlockSpec(memory_space=pltpu.SMEM),
                  pl.BlockSpec((1, BLOCK, qw), lambda b, n: (b, n, 0)),
                  pl.BlockSpec((1, BLOCK, kw), prev),
                  pl.BlockSpec((1, BLOCK, kw), cur),
                  pl.BlockSpec((1, BLOCK, kw), prev_v),
                  pl.BlockSpec((1, BLOCK, kw), cur_v),
                  pl.BlockSpec((1, LANES), lambda b, n: (0, 0)),
                  pl.BlockSpec((1, LANES), lambda b, n: (0, 0)),
                  pl.BlockSpec((LANES, LANES), lambda b, n: (0, 0))],
        out_specs=pl.BlockSpec((1, BLOCK, qw), lambda b, n: (b, n, 0)),
        out_shape=jax.ShapeDtypeStruct((batch, seq, qw), BF16),
        compiler_params=_params("parallel", "arbitrary"),
        name="sw_attention",
    )(sinks.astype(F32), qkv, qkv, qkv, qkv, qkv, vec(q_gain), vec(k_gain), bd)


def _sw_qkv_weight(w):
    d = w.shape[0]
    qw = SW_Q_HEADS * HEAD_DIM
    kvw = SW_KV_HEADS * HEAD_DIM
    dup = lambda a: jnp.broadcast_to(
        a.reshape(d, SW_KV_HEADS, 1, HEAD_DIM), (d, SW_KV_HEADS, 2, HEAD_DIM)).reshape(d, 2 * kvw)
    return jnp.concatenate([w[:, :qw], dup(w[:, qw:qw + kvw]), dup(w[:, qw + kvw:])], axis=1)


def _proj_kernel(o_ref, wo_ref, h_ref, g_ref, wq_ref, hn_ref, xn_ref, q_ref):
    hn = h_ref[...] + jnp.dot(o_ref[...], wo_ref[...], preferred_element_type=F32)
    hn_ref[...] = hn
    xn = (hn * _rms_scale(hn) * g_ref[...]).astype(BF16)
    xn_ref[...] = xn
    q_ref[...] = jnp.dot(xn, wq_ref[...], preferred_element_type=F32).astype(BF16)


def proj_residual_query(o, wo, h, g, wq, tm=512):
    t, d = h.shape
    nq = wq.shape[1]
    tm = min(tm, t)
    row = lambda i: (i, 0)
    fixed = lambda i: (0, 0)
    return pl.pallas_call(
        _proj_kernel,
        grid=(t // tm,),
        in_specs=[pl.BlockSpec((tm, d), row), pl.BlockSpec((d, d), fixed),
                  pl.BlockSpec((tm, d), row), pl.BlockSpec((1, d), fixed),
                  pl.BlockSpec((d, nq), fixed)],
        out_specs=[pl.BlockSpec((tm, d), row), pl.BlockSpec((tm, d), row),
                   pl.BlockSpec((tm, nq), row)],
        out_shape=[jax.ShapeDtypeStruct((t, d), F32), jax.ShapeDtypeStruct((t, d), BF16),
                   jax.ShapeDtypeStruct((t, nq), BF16)],
        compiler_params=_params("parallel"),
        name="proj_residual_query",
    )(o, wo, h, g.reshape(1, d), wq)


def _cand_tables():
    k = PEER_TOPK
    flat = np.zeros((80,), np.float32)
    neg = np.zeros((80,), np.float32)
    for a in range(k):
        flat[a] = a * k
    for b in range(1, 8):
        for a in range(8):
            r = 16 + 8 * (b - 1) + a
            flat[r] = a * k + b
            if (a + 1) * (b + 1) > k:
                neg[r] = NEG_INF
    for b in range(8, k):
        flat[72 + b - 8] = b
    tile = lambda v: jnp.asarray(np.repeat(v[:, None], LANES, axis=1))
    return tile(flat), tile(neg)


def _dup_words(x):
    b = lax.bitcast_convert_type(x.astype(BF16).astype(F32), jnp.uint32)
    return b | (b >> 16)


def _pair_words(x, stage):
    half = x.shape[0] // 2
    stage[...] = x.astype(BF16).astype(F32)
    ev = lax.bitcast_convert_type(stage[pl.ds(0, half, stride=2), :], jnp.uint32)
    od = lax.bitcast_convert_type(stage[pl.ds(1, half, stride=2), :], jnp.uint32)
    return (ev >> 16) | (od & jnp.uint32(0xFFFF0000))


def _topk_kernel(q_ref, sk_ref, flat_ref, neg_ref, ns_ref, e1_ref, gt_ref,
                 s_scr, rk_scr, tv_scr, cand_scr, sel_scr, stage):
    tt = LANES
    nl = 2 * PEER_HEADS
    iota = lax.broadcasted_iota(jnp.int32, (PEER_KEYS, tt), 0).astype(F32)
    big = float(4 * PEER_KEYS * PEER_KEYS)
    unranked = float(PEER_TOPK)

    def scores(l):
        return lax.dot_general(sk_ref[l], q_ref[:, l * PEER_HALF:(l + 1) * PEER_HALF],
                               _NT, preferred_element_type=F32)

    def l1_reset(l):
        s_scr[l] = scores(l)
        rk_scr[l] = jnp.full((PEER_KEYS, tt), unranked, F32)

    def l1_round(l, r, exact, unroll=1):
        s = s_scr[l]
        rk = rk_scr[l]
        for n in range(unroll):
            rr = unroll * r + n
            m = jnp.max(s, axis=0, keepdims=True)
            hit = s == m
            if exact:
                hit = iota == jnp.min(jnp.where(hit, iota, big), axis=0, keepdims=True)
            s = jnp.where(hit, NEG_INF, s)
            rk = jnp.where(hit, jnp.asarray(rr, F32), rk)
            tv_scr[l, pl.ds(rr, 1), :] = m
        s_scr[l] = s
        rk_scr[l] = rk

    def miscount(x):
        return jnp.abs(jnp.sum(x, axis=0, keepdims=True) - float(PEER_TOPK))

    def redo_where_miscounted(counts, redo):
        worst = counts[0]
        for c in counts[1:]:
            worst = jnp.maximum(worst, c)

        @pl.when(jnp.max(worst) > 0.0)
        def _():
            for idx, c in enumerate(counts):
                pl.when(jnp.max(c) > 0.0)(functools.partial(redo, idx))

    for l in range(nl):
        l1_reset(l)

    def l1_fast(r, c):
        for l in range(nl):
            l1_round(l, r, False, unroll=L1_UNROLL)
        return c

    lax.fori_loop(0, PEER_TOPK // L1_UNROLL, l1_fast, 0)

    def l1_redo(l):
        l1_reset(l)

        def l1_exact(r, c):
            l1_round(l, r, True)
            return c

        lax.fori_loop(0, PEER_TOPK, l1_exact, 0)

    redo_where_miscounted(
        [miscount(jnp.where(rk_scr[l] < unranked, 1.0, 0.0)) for l in range(nl)], l1_redo)

    flat = flat_ref[...]

    def l2_reset(h):
        tv1 = tv_scr[2 * h]
        tv2 = tv_scr[2 * h + 1]
        parts = [tv1 + tv2[0:1, :]]
        for b in range(1, 8):
            parts.append(tv1[0:8, :] + tv2[b:b + 1, :])
        parts.append(tv1[0:1, :] + tv2[8:16, :])
        cand_scr[h] = jnp.concatenate(parts, axis=0) + neg_ref[...]
        sel_scr[h] = jnp.zeros((80, tt), F32)

    def l2_round(h, exact, unroll=1):
        cd = cand_scr[h]
        sel = sel_scr[h]
        for _ in range(unroll):
            m = jnp.max(cd, axis=0, keepdims=True)
            hit = cd == m
            if exact:
                hit = flat == jnp.min(jnp.where(hit, flat, big), axis=0, keepdims=True)
            cd = jnp.where(hit, NEG_INF, cd)
            sel = jnp.where(hit, 1.0, sel)
        cand_scr[h] = cd
        sel_scr[h] = sel

    for h in range(PEER_HEADS):
        l2_reset(h)

    def l2_fast(r, c):
        for h in range(PEER_HEADS):
            l2_round(h, False, unroll=L1_UNROLL)
        return c

    lax.fori_loop(0, PEER_TOPK // L1_UNROLL, l2_fast, 0)

    def l2_redo(h):
        l2_reset(h)

        def l2_exact(r, c):
            l2_round(h, True)
            return c

        lax.fori_loop(0, PEER_TOPK, l2_exact, 0)

    redo_where_miscounted([miscount(sel_scr[h]) for h in range(PEER_HEADS)], l2_redo)

    row8 = lax.broadcasted_iota(jnp.int32, (8, tt), 0)
    for h in range(PEER_HEADS):
        tv1 = tv_scr[2 * h]
        tv2 = tv_scr[2 * h + 1]
        sel = sel_scr[h]
        x1 = jnp.exp(tv1 - tv1[0:1, :])
        x2 = jnp.exp(tv2 - tv2[0:1, :])
        eparts = [x1 * x2[0:1, :]]
        for b in range(1, 8):
            eparts.append(x1[0:8, :] * x2[b:b + 1, :])
        eparts.append(x1[0:1, :] * x2[8:16, :])
        z = jnp.sum(sel * jnp.concatenate(eparts, axis=0), axis=0, keepdims=True)
        n_lo = sel[0:8, :]
        for b in range(1, 8):
            n_lo = n_lo + sel[16 + 8 * (b - 1):24 + 8 * (b - 1), :]
        n_lo = n_lo + jnp.where(row8 == 0, jnp.sum(sel[72:80, :], axis=0, keepdims=True), 0.0)
        s1 = lax.dot_general(sk_ref[2 * h], q_ref[:, (2 * h) * PEER_HALF:(2 * h + 1) * PEER_HALF],
                             _NT, preferred_element_type=F32)
        s2 = lax.dot_general(sk_ref[2 * h + 1],
                             q_ref[:, (2 * h + 1) * PEER_HALF:(2 * h + 2) * PEER_HALF],
                             _NT, preferred_element_type=F32)
        rank1 = rk_scr[2 * h]
        r2 = rk_scr[2 * h + 1]
        a0 = jnp.sum(sel[0:PEER_TOPK, :], axis=0, keepdims=True)
        ns = jnp.where(rank1 < a0, 1.0, 0.0)
        for a in range(8):
            ns = jnp.where(rank1 == float(a), n_lo[a:a + 1, :], ns)
        ns_ref[h] = _dup_words(ns)
        e1_ref[h] = _dup_words(jnp.exp(s1 - tv1[0:1, :]) / z)
        for w, tab in enumerate((r2, jnp.exp(s2 - tv2[0:1, :]))):
            words = _pair_words(tab, stage)
            for k in range(PEER_KEYS // 16):
                gt_ref[_gt_row(k, h, w):_gt_row(k, h, w) + 8, :] = words[8 * k:8 * k + 8, :]


GT_ROWS = (PEER_KEYS // 16) * PEER_HEADS * 2 * 8


def _gt_row(k, h, w):
    return ((k * PEER_HEADS + h) * 2 + w) * 8


def peer_topk(q, sub_keys):
    t = q.shape[0]
    tt = LANES
    nl = 2 * PEER_HEADS
    flat, neg = _cand_tables()
    out_w = jax.ShapeDtypeStruct((PEER_HEADS, PEER_KEYS, t), jnp.uint32)
    out_g = jax.ShapeDtypeStruct((t // tt * GT_ROWS, tt), jnp.uint32)
    ospec = pl.BlockSpec((PEER_HEADS, PEER_KEYS, tt), lambda i: (0, 0, i))
    return pl.pallas_call(
        _topk_kernel,
        grid=(t // tt,),
        in_specs=[pl.BlockSpec((tt, nl * PEER_HALF), lambda i: (i, 0)),
                  pl.BlockSpec((nl, PEER_KEYS, PEER_HALF), lambda i: (0, 0, 0)),
                  pl.BlockSpec((80, tt), lambda i: (0, 0)),
                  pl.BlockSpec((80, tt), lambda i: (0, 0))],
        out_specs=[ospec, ospec, pl.BlockSpec((GT_ROWS, tt), lambda i: (i, 0))],
        out_shape=[out_w, out_w, out_g],
        scratch_shapes=[pltpu.VMEM((nl, PEER_KEYS, tt), F32),
                        pltpu.VMEM((nl, PEER_KEYS, tt), F32),
                        pltpu.VMEM((nl, PEER_TOPK, tt), F32),
                        pltpu.VMEM((PEER_HEADS, 80, tt), F32),
                        pltpu.VMEM((PEER_HEADS, 80, tt), F32),
                        pltpu.VMEM((PEER_KEYS, tt), F32)],
        compiler_params=_params("parallel"),
        name="peer_topk",
    )(q, sub_keys, flat, neg)


GELU_C = float(np.sqrt(2.0 / np.pi))
L1_UNROLL = 2


def _peer_kernel(xn_ref, u_ref, vt_ref, ns_ref, e1_ref, gt_ref, h_ref, o_ref,
                 acc_ref, hta_ref, htb_ref, cfa_ref, cfb_ref, *, et, tt):
    j = pl.program_id(1)
    rows = et // PEER_KEYS
    mb = 2 * LANES
    d = xn_ref.shape[1]

    @pl.when(j == 0)
    def _():
        acc_ref[...] = jnp.zeros_like(acc_ref)

    pk = 16
    zero = jnp.zeros((pk, LANES), BF16)

    def bcast(words, r):
        return pltpu.bitcast(jnp.broadcast_to(words[r:r + 1, :], (pk // 2, LANES)), BF16)

    pieces = [(rb, nb) for rb in range(et // mb) for nb in range(tt // mb)]

    def first_piece(tile, ht_ref, p):
        rb, nb = pieces[p]
        ht_ref[rb * mb:(rb + 1) * mb, nb * mb:(nb + 1) * mb] = lax.dot_general(
            u_ref[tile * et + rb * mb:tile * et + (rb + 1) * mb, :],
            xn_ref[nb * mb:(nb + 1) * mb, :], _NT, preferred_element_type=F32)

    def second_piece(tile, cf_ref, p):
        rb, nb = pieces[p]
        acc_ref[rb * mb:(rb + 1) * mb, nb * mb:(nb + 1) * mb] += jnp.dot(
            vt_ref[tile, rb * mb:(rb + 1) * mb, :], cf_ref[:, nb * mb:(nb + 1) * mb],
            preferred_element_type=F32)

    def gate_row(tile, ht_ref, cf_ref, r):
        base = pl.multiple_of((2 * j + tile) * rows, rows)
        for c in range(tt // LANES):
            cs = slice(c * LANES, (c + 1) * LANES)
            ns = [bcast(ns_ref[h, pl.ds(base, rows), cs], r) for h in range(PEER_HEADS)]
            e1 = [bcast(e1_ref[h, pl.ds(base, rows), cs], r) for h in range(PEER_HEADS)]
            for k in range(PEER_KEYS // pk):
                g = zero
                for h in range(PEER_HEADS):
                    r2 = pltpu.bitcast(gt_ref[pl.ds(c * GT_ROWS + _gt_row(k, h, 0), 8), :], BF16)
                    e2 = pltpu.bitcast(gt_ref[pl.ds(c * GT_ROWS + _gt_row(k, h, 1), 8), :], BF16)
                    g = g + jnp.where(r2 < ns[h], e2, zero) * e1[h]
                es = slice(r * PEER_KEYS + k * pk, r * PEER_KEYS + (k + 1) * pk)
                x = ht_ref[es, cs]
                t = jnp.tanh(x * (GELU_C + (GELU_C * 0.044715) * (x * x))).astype(BF16)
                hx = x.astype(BF16) * 0.5
                cf_ref[es, cs] = g * (hx + hx * t)

    npiece = len(pieces)
    assert npiece == rows and et == d
    for p in range(npiece):
        first_piece(0, hta_ref, p)
    for r in range(rows):
        gate_row(0, hta_ref, cfa_ref, r)
        first_piece(1, htb_ref, r)
    for r in range(rows):
        second_piece(0, cfa_ref, r)
        gate_row(1, htb_ref, cfb_ref, r)
    for p in range(npiece):
        second_piece(1, cfb_ref, p)

    @pl.when(j == pl.num_programs(1) - 1)
    def _():
        o_ref[...] = h_ref[...] + acc_ref[...].T


def peer_mix(xn, u, vt, ns, e1, gt, h, et=PEER_ET, tt=512):
    t, d = h.shape
    ne = u.shape[0]
    tt = min(tt, t)
    assert et % (8 * PEER_KEYS) == 0 and ne % (2 * et) == 0 and t % tt == 0
    once = pl.Buffered(1)
    gspec = pl.BlockSpec((PEER_HEADS, PEER_KEYS, tt), lambda i, j: (0, 0, i), pipeline_mode=once)
    return pl.pallas_call(
        functools.partial(_peer_kernel, et=et, tt=tt),
        grid=(t // tt, ne // (2 * et)),
        in_specs=[pl.BlockSpec((tt, d), lambda i, j: (i, 0), pipeline_mode=once),
                  pl.BlockSpec((2 * et, d), lambda i, j: (j, 0)),
                  pl.BlockSpec((2, d, et), lambda i, j: (j, 0, 0)),
                  gspec, gspec,
                  pl.BlockSpec((tt // LANES * GT_ROWS, LANES), lambda i, j: (i, 0), pipeline_mode=once),
                  pl.BlockSpec((tt, d), lambda i, j: (i, 0), pipeline_mode=once)],
        out_specs=pl.BlockSpec((tt, d), lambda i, j: (i, 0)),
        out_shape=jax.ShapeDtypeStruct((t, d), F32),
        scratch_shapes=[pltpu.VMEM((d, tt), F32),
                        pltpu.VMEM((et, tt), F32), pltpu.VMEM((et, tt), F32),
                        pltpu.VMEM((et, tt), BF16), pltpu.VMEM((et, tt), BF16)],
        compiler_params=_params("parallel", "arbitrary"),
        name="peer_mix",
    )(xn, u, vt, ns, e1, gt, h)


def _vt_tiles(v, et=PEER_ET):
    ne, d = v.shape
    return v.astype(BF16).reshape(ne // et, et, d).transpose(0, 2, 1)


def _peer_block(o, wo, h, g, wq, sub_keys, u, v):
    hn, xn, q = proj_residual_query(o, wo.astype(BF16), h, g, wq.astype(BF16))
    sk = sub_keys.reshape(2 * PEER_HEADS, PEER_KEYS, PEER_HALF).astype(BF16)
    ns, e1, gt = peer_topk(q, sk)
    return peer_mix(xn, u.astype(BF16), _vt_tiles(v), ns, e1, gt, hn)


def kernel(x, sb_norm, sb_w_qkv, sb_w_o, sw_norm, sw_w_qkv, sw_q_gain, sw_k_gain, sw_sinks, sw_w_o,
           ffn_norm, peer_w_query, peer_sub_keys, peer_u, peer_v):
    batch, seq, d = x.shape
    t = batch * seq
    h = x.reshape(t, d)

    qkv = norm_matmul(h, sb_norm[0], sb_w_qkv[0].astype(BF16), BF16)
    o = sb_attention(qkv.reshape(batch, seq, -1), batch, seq).reshape(t, -1)
    h = _peer_block(o, sb_w_o[0], h, ffn_norm[0], peer_w_query[0], peer_sub_keys[0],
                    peer_u[0], peer_v[0])

    qkv = norm_matmul(h, sw_norm[0], _sw_qkv_weight(sw_w_qkv[0]).astype(BF16), F32)
    o = sw_attention(qkv.reshape(batch, seq, -1), sw_sinks[0], sw_q_gain[0], sw_k_gain[0],
                     batch, seq).reshape(t, -1)
    h = _peer_block(o, sw_w_o[0], h, ffn_norm[1], peer_w_query[1], peer_sub_keys[1],
                    peer_u[1], peer_v[1])
    return h.reshape(batch, seq, d)
```

```python
import functools

import numpy as np
import jax
import jax.numpy as jnp
from jax import lax
from jax.experimental import pallas as pl
from jax.experimental.pallas import tpu as pltpu

F32 = jnp.float32
BF16 = jnp.bfloat16

D_MODEL = 1024
HEAD_DIM = 64
SB_HEADS = 16
SW_Q_HEADS = 16
SW_KV_HEADS = 4
BLOCK = 128
LANES = 128
PEER_HEADS = 8
PEER_KEYS = 128
PEER_HALF = 128
PEER_TOPK = 16
RMS_EPS = 1e-6
ATTN_SCALE = HEAD_DIM ** -0.5
SB_DEAD_LOG = -110.0
SB_PAIRS = 4
NEG_INF = float("-inf")
VMEM_LIMIT = 56 * 1024 * 1024
PEER_ET = 1024

_NT = (((1,), (1,)), ((), ()))


def _params(*sem, flags=None):
    return pltpu.CompilerParams(dimension_semantics=sem, vmem_limit_bytes=VMEM_LIMIT, flags=flags)


def _rms_scale(x):
    return lax.rsqrt(jnp.mean(x * x, axis=-1, keepdims=True) + RMS_EPS)


def _norm_matmul_kernel(x_ref, g_ref, w_ref, o_ref):
    x = x_ref[...]
    xn = (x * _rms_scale(x) * g_ref[...]).astype(BF16)
    o_ref[...] = jnp.dot(xn, w_ref[...], preferred_element_type=F32).astype(o_ref.dtype)


def norm_matmul(x, g, w, out_dtype, tm=512):
    t, d = x.shape
    n = w.shape[1]
    tm = min(tm, t)
    return pl.pallas_call(
        _norm_matmul_kernel,
        grid=(t // tm,),
        in_specs=[pl.BlockSpec((tm, d), lambda i: (i, 0)),
                  pl.BlockSpec((1, d), lambda i: (0, 0)),
                  pl.BlockSpec((d, n), lambda i: (0, 0))],
        out_specs=pl.BlockSpec((tm, n), lambda i: (i, 0)),
        out_shape=jax.ShapeDtypeStruct((t, n), out_dtype),
        compiler_params=_params("parallel"),
        name="norm_matmul",
    )(x, g.reshape(1, d), w)


def _sb_kernel(q_ref, k_ref, v_ref, tri1_ref, tri2_ref, o_ref):
    i = pl.program_id(2)
    tq = BLOCK
    nchain = 4 * SB_PAIRS
    lane = lax.broadcasted_iota(jnp.int32, (tq, LANES), 1)
    iota = lambda w, d: lax.broadcasted_iota(jnp.int32, (tq, w), d)
    causal1 = iota(tq, 1) < iota(tq, 0)
    causal2 = iota(2 * tq, 1) < iota(2 * tq, 0) + tq

    def spans(chains, start, w, carries, accs, mask):
        ls = lambda c: slice((c // 4) * LANES, (c // 4 + 1) * LANES)
        tri = tri1_ref[...] if w == tq else tri2_ref[...]
        zs = [lax.dot_general(qs[c], k_ref[0, pl.ds(start, w), ls(c)], _NT,
                              preferred_element_type=F32) for c in chains]
        lfs = [-(jnp.maximum(z, 0.0) + jnp.log1p(jnp.exp(-jnp.abs(z)))) for z in zs]
        wts, tots = [], []
        for lf in lfs:
            lfm = lf if mask is None else jnp.where(mask, lf, 0.0)
            hi = lfm.astype(BF16)
            lo = (lfm - hi.astype(F32)).astype(BF16)
            wt = jnp.dot(jnp.concatenate([hi, lo], axis=1), tri, preferred_element_type=F32)
            wts.append(wt)
            tots.append(jnp.broadcast_to(wt[:, 0:1] + lfm[:, 0:1], (tq, LANES)))
        out_c, out_a = [], []
        for n, c in enumerate(chains):
            carry = carries[n]
            cw = carry if w == tq else jnp.concatenate([carry, carry], axis=1)
            a = jnp.exp(zs[n] + lfs[n] + wts[n] + cw)
            if mask is not None:
                a = jnp.where(mask, a, 0.0)
            out_a.append(accs[n] + jnp.dot(a.astype(BF16), v_ref[0, pl.ds(start, w), ls(c)],
                                           preferred_element_type=F32))
            out_c.append(carry + tots[n])
        return out_c, out_a

    qs = []
    for pp in range(SB_PAIRS):
        for u in range(2):
            q2 = q_ref[0, u * tq:(u + 1) * tq, pp * LANES:(pp + 1) * LANES]
            q2 = (q2.astype(F32) * ATTN_SCALE).astype(BF16)
            for hh in range(2):
                hmask = (lane < HEAD_DIM) if hh == 0 else (lane >= HEAD_DIM)
                qs.append(jnp.where(hmask, q2, jnp.zeros_like(q2)))
    zero = jnp.zeros((tq, LANES), F32)
    base = pl.multiple_of(i * (2 * tq), 2 * tq)
    first = [c for c in range(nchain) if (c // 2) % 2 == 0]
    second = [c for c in range(nchain) if (c // 2) % 2 == 1]
    zeros = [zero] * len(first)
    c1, a1 = spans(first, base, tq, zeros, zeros, causal1)
    c2, a2 = spans(second, base, 2 * tq, zeros, zeros, causal2)
    carries, accs = [None] * nchain, [None] * nchain
    for n, c in enumerate(first):
        carries[c], accs[c] = c1[n], a1[n]
    for n, c in enumerate(second):
        carries[c], accs[c] = c2[n], a2[n]

    def alive(cs):
        m = cs[0]
        for x in cs[1:]:
            m = jnp.maximum(m, x)
        return jnp.max(m)

    def cond(st):
        r, _, _, m = st
        return jnp.logical_and(r < i, m > SB_DEAD_LOG)

    def body(st):
        r, cs, ac, _ = st
        start = pl.multiple_of((i - 1 - r) * (2 * tq), 2 * tq)
        cs, ac = spans(list(range(nchain)), start, 2 * tq, list(cs), list(ac), None)
        return r + 1, tuple(cs), tuple(ac), alive(cs)

    _, _, accs, _ = lax.while_loop(
        cond, body, (jnp.int32(0), tuple(carries), tuple(accs), alive(carries)))
    for pp in range(SB_PAIRS):
        for u in range(2):
            c = 4 * pp + 2 * u
            o_ref[0, u * tq:(u + 1) * tq, pp * LANES:(pp + 1) * LANES] = jnp.where(
                lane < HEAD_DIM, accs[c], accs[c + 1]).astype(o_ref.dtype)


def _sb_tri(w):
    r = np.arange(2 * w)[:, None] % w
    c = np.arange(w)[None, :]
    return jnp.asarray(r > c, dtype=BF16)


def sb_attention(qkv, batch, seq):
    ngroup = SB_HEADS * HEAD_DIM // (SB_PAIRS * LANES)
    width = SB_PAIRS * LANES
    return pl.pallas_call(
        _sb_kernel,
        grid=(batch, ngroup, seq // (2 * BLOCK)),
        in_specs=[pl.BlockSpec((1, 2 * BLOCK, width), lambda b, p, i: (b, i, p)),
                  pl.BlockSpec((1, seq, width), lambda b, p, i: (b, 0, ngroup + p)),
                  pl.BlockSpec((1, seq, width), lambda b, p, i: (b, 0, 2 * ngroup + p)),
                  pl.BlockSpec((2 * BLOCK, BLOCK), lambda b, p, i: (0, 0)),
                  pl.BlockSpec((4 * BLOCK, 2 * BLOCK), lambda b, p, i: (0, 0))],
        out_specs=pl.BlockSpec((1, 2 * BLOCK, width), lambda b, p, i: (b, i, p)),
        out_shape=jax.ShapeDtypeStruct((batch, seq, SB_HEADS * HEAD_DIM), BF16),
        compiler_params=_params("parallel", "parallel", "arbitrary"),
        name="sb_attention",
    )(qkv, qkv, qkv, _sb_tri(BLOCK), _sb_tri(2 * BLOCK))


def _alibi_slopes(n_heads):
    return [float(v) for v in np.asarray(
        2.0 ** (-8.0 * np.arange(1, n_heads + 1) / n_heads), dtype=np.float32)]


def _sw_kernel(sink_ref, q_ref, kp_ref, kc_ref, vp_ref, vc_ref, qg_ref, kg_ref, bd_ref, o_ref):
    n = pl.program_id(1)
    lane = lax.broadcasted_iota(jnp.int32, (BLOCK, LANES), 1)
    ii = lax.broadcasted_iota(jnp.int32, (BLOCK, 2 * BLOCK), 0)
    jj = lax.broadcasted_iota(jnp.int32, (BLOCK, 2 * BLOCK), 1)
    dist = BLOCK + ii - jj
    valid = (dist >= 0) & (dist < BLOCK) & jnp.logical_or(jj >= BLOCK, n > 0)
    distf = dist.astype(F32)
    slopes = _alibi_slopes(SW_Q_HEADS)
    bd = bd_ref[...]
    group = SW_Q_HEADS // SW_KV_HEADS
    heads = range(SW_Q_HEADS)
    kns, v2s = [], []
    for kk in range(SW_KV_HEADS):
        cs = slice(kk * LANES, (kk + 1) * LANES)
        k2 = jnp.concatenate([kp_ref[0, :, cs], kc_ref[0, :, cs]], axis=0)
        kns.append((k2 * _rms_scale(k2) * kg_ref[...]).astype(BF16))
        v2s.append(jnp.concatenate([vp_ref[0, :, cs], vc_ref[0, :, cs]], axis=0).astype(BF16))
    qns = []
    for pair in range(SW_Q_HEADS // 2):
        qq = q_ref[0, :, pair * LANES:(pair + 1) * LANES]
        sq = qq * qq
        hi = sq.astype(BF16)
        lo = (sq - hi.astype(F32)).astype(BF16)
        ssq = (jnp.dot(hi, bd, preferred_element_type=F32)
               + jnp.dot(lo, bd, preferred_element_type=F32))
        qn = qq * lax.rsqrt(ssq * (1.0 / HEAD_DIM) + RMS_EPS) * qg_ref[...]
        qns.append((qn * ATTN_SCALE).astype(BF16))
    ss = []
    for head in heads:
        qn = qns[head // 2]
        hmask = (lane < HEAD_DIM) if head % 2 == 0 else (lane >= HEAD_DIM)
        qh = jnp.where(hmask, qn, jnp.zeros_like(qn))
        s = lax.dot_general(qh, kns[head // group], _NT, preferred_element_type=F32)
        ss.append(jnp.where(valid, s - slopes[head] * distf, NEG_INF))
    ms = [jnp.maximum(jnp.max(ss[head], axis=-1, keepdims=True), sink_ref[head]) for head in heads]
    ps = [jnp.exp(ss[head] - ms[head]) for head in heads]
    denoms = [jnp.sum(ps[head], axis=-1, keepdims=True) + jnp.exp(sink_ref[head] - ms[head])
              for head in heads]
    outs = [jnp.dot(ps[head].astype(BF16), v2s[head // group], preferred_element_type=F32)
            / denoms[head] for head in heads]
    for pair in range(SW_Q_HEADS // 2):
        o_ref[0, :, pair * LANES:(pair + 1) * LANES] = jnp.where(
            lane < HEAD_DIM, outs[2 * pair], outs[2 * pair + 1]).astype(o_ref.dtype)


def sw_attention(qkv, sinks, q_gain, k_gain, batch, seq):
    nb = seq // BLOCK
    qw = SW_Q_HEADS * HEAD_DIM
    kw = SW_KV_HEADS * LANES
    bd = jnp.asarray(np.kron(np.eye(2), np.ones((HEAD_DIM, HEAD_DIM))), dtype=BF16)
    prev = lambda b, n: (b, jnp.maximum(n - 1, 0), qw // kw)
    cur = lambda b, n: (b, n, qw // kw)
    prev_v = lambda b, n: (b, jnp.maximum(n - 1, 0), qw // kw + 1)
    cur_v = lambda b, n: (b, n, qw // kw + 1)
    vec = lambda g: jnp.tile(g.astype(F32), 2).reshape(1, LANES)
    return pl.pallas_call(
        _sw_kernel,
        grid=(batch, nb),
        in_specs=[pl.BlockSpec(memory_space=pltpu.SMEM),
                  pl.BlockSpec((1, BLOCK, qw), lambda b, n: (b, n, 0)),
                  pl.BlockSpec((1, BLOCK, kw), prev),
                  pl.BlockSpec((1, BLOCK, kw), cur),
                  pl.BlockSpec((1, BLOCK, kw), prev_v),
                  pl.BlockSpec((1, BLOCK, kw), cur_v),
                  pl.BlockSpec((1, LANES), lambda b, n: (0, 0)),
                  pl.BlockSpec((1, LANES), lambda b, n: (0, 0)),
                  pl.BlockSpec((LANES, LANES), lambda b, n: (0, 0))],
        out_specs=pl.BlockSpec((1, BLOCK, qw), lambda b, n: (b, n, 0)),
        out_shape=jax.ShapeDtypeStruct((batch, seq, qw), BF16),
        compiler_params=_params("parallel", "arbitrary"),
        name="sw_attention",
    )(sinks.astype(F32), qkv, qkv, qkv, qkv, qkv, vec(q_gain), vec(k_gain), bd)


def _sw_qkv_weight(w):
    d = w.shape[0]
    qw = SW_Q_HEADS * HEAD_DIM
    kvw = SW_KV_HEADS * HEAD_DIM
    dup = lambda a: jnp.broadcast_to(
        a.reshape(d, SW_KV_HEADS, 1, HEAD_DIM), (d, SW_KV_HEADS, 2, HEAD_DIM)).reshape(d, 2 * kvw)
    return jnp.concatenate([w[:, :qw], dup(w[:, qw:qw + kvw]), dup(w[:, qw + kvw:])], axis=1)


def _proj_kernel(o_ref, wo_ref, h_ref, g_ref, wq_ref, hn_ref, xn_ref, q_ref):
    hn = h_ref[...] + jnp.dot(o_ref[...], wo_ref[...], preferred_element_type=F32)
    hn_ref[...] = hn
    xn = (hn * _rms_scale(hn) * g_ref[...]).astype(BF16)
    xn_ref[...] = xn
    q_ref[...] = jnp.dot(xn, wq_ref[...], preferred_element_type=F32).astype(BF16)


def proj_residual_query(o, wo, h, g, wq, tm=512):
    t, d = h.shape
    nq = wq.shape[1]
    tm = min(tm, t)
    row = lambda i: (i, 0)
    fixed = lambda i: (0, 0)
    return pl.pallas_call(
        _proj_kernel,
        grid=(t // tm,),
        in_specs=[pl.BlockSpec((tm, d), row), pl.BlockSpec((d, d), fixed),
                  pl.BlockSpec((tm, d), row), pl.BlockSpec((1, d), fixed),
                  pl.BlockSpec((d, nq), fixed)],
        out_specs=[pl.BlockSpec((tm, d), row), pl.BlockSpec((tm, d), row),
                   pl.BlockSpec((tm, nq), row)],
        out_shape=[jax.ShapeDtypeStruct((t, d), F32), jax.ShapeDtypeStruct((t, d), BF16),
                   jax.ShapeDtypeStruct((t, nq), BF16)],
        compiler_params=_params("parallel"),
        name="proj_residual_query",
    )(o, wo, h, g.reshape(1, d), wq)


def _cand_tables():
    k = PEER_TOPK
    flat = np.zeros((80,), np.float32)
    neg = np.zeros((80,), np.float32)
    for a in range(k):
        flat[a] = a * k
    for b in range(1, 8):
        for a in range(8):
            r = 16 + 8 * (b - 1) + a
            flat[r] = a * k + b
            if (a + 1) * (b + 1) > k:
                neg[r] = NEG_INF
    for b in range(8, k):
        flat[72 + b - 8] = b
    tile = lambda v: jnp.asarray(np.repeat(v[:, None], LANES, axis=1))
    return tile(flat), tile(neg)


def _dup_words(x):
    b = lax.bitcast_convert_type(x.astype(BF16).astype(F32), jnp.uint32)
    return b | (b >> 16)


def _pair_words(x, stage):
    half = x.shape[0] // 2
    stage[...] = x.astype(BF16).astype(F32)
    ev = lax.bitcast_convert_type(stage[pl.ds(0, half, stride=2), :], jnp.uint32)
    od = lax.bitcast_convert_type(stage[pl.ds(1, half, stride=2), :], jnp.uint32)
    return (ev >> 16) | (od & jnp.uint32(0xFFFF0000))


def _topk_kernel(q_ref, sk_ref, flat_ref, neg_ref, ns_ref, e1_ref, gt_ref,
                 s_scr, rk_scr, tv_scr, cand_scr, sel_scr, stage):
    tt = LANES
    nl = 2 * PEER_HEADS
    iota = lax.broadcasted_iota(jnp.int32, (PEER_KEYS, tt), 0).astype(F32)
    big = float(4 * PEER_KEYS * PEER_KEYS)
    unranked = float(PEER_TOPK)

    def scores(l):
        return lax.dot_general(sk_ref[l], q_ref[:, l * PEER_HALF:(l + 1) * PEER_HALF],
                               _NT, preferred_element_type=F32)

    def l1_reset(l):
        s_scr[l] = scores(l)
        rk_scr[l] = jnp.full((PEER_KEYS, tt), unranked, F32)

    def l1_round(l, r, exact, unroll=1):
        s = s_scr[l]
        rk = rk_scr[l]
        for n in range(unroll):
            rr = unroll * r + n
            m = jnp.max(s, axis=0, keepdims=True)
            hit = s == m
            if exact:
                hit = iota == jnp.min(jnp.where(hit, iota, big), axis=0, keepdims=True)
            s = jnp.where(hit, NEG_INF, s)
            rk = jnp.where(hit, jnp.asarray(rr, F32), rk)
            tv_scr[l, pl.ds(rr, 1), :] = m
        s_scr[l] = s
        rk_scr[l] = rk

    def miscount(x):
        return jnp.abs(jnp.sum(x, axis=0, keepdims=True) - float(PEER_TOPK))

    def redo_where_miscounted(counts, redo):
        worst = counts[0]
        for c in counts[1:]:
            worst = jnp.maximum(worst, c)

        @pl.when(jnp.max(worst) > 0.0)
        def _():
            for idx, c in enumerate(counts):
                pl.when(jnp.max(c) > 0.0)(functools.partial(redo, idx))

    for l in range(nl):
        l1_reset(l)

    def l1_fast(r, c):
        for l in range(nl):
            l1_round(l, r, False, unroll=L1_UNROLL)
        return c

    lax.fori_loop(0, PEER_TOPK // L1_UNROLL, l1_fast, 0)

    def l1_redo(l):
        l1_reset(l)

        def l1_exact(r, c):
            l1_round(l, r, True)
            return c

        lax.fori_loop(0, PEER_TOPK, l1_exact, 0)

    redo_where_miscounted(
        [miscount(jnp.where(rk_scr[l] < unranked, 1.0, 0.0)) for l in range(nl)], l1_redo)

    flat = flat_ref[...]

    def l2_reset(h):
        tv1 = tv_scr[2 * h]
        tv2 = tv_scr[2 * h + 1]
        parts = [tv1 + tv2[0:1, :]]
        for b in range(1, 8):
            parts.append(tv1[0:8, :] + tv2[b:b + 1, :])
        parts.append(tv1[0:1, :] + tv2[8:16, :])
        cand_scr[h] = jnp.concatenate(parts, axis=0) + neg_ref[...]
        sel_scr[h] = jnp.zeros((80, tt), F32)

    def l2_round(h, exact, unroll=1):
        cd = cand_scr[h]
        sel = sel_scr[h]
        for _ in range(unroll):
            m = jnp.max(cd, axis=0, keepdims=True)
            hit = cd == m
            if exact:
                hit = flat == jnp.min(jnp.where(hit, flat, big), axis=0, keepdims=True)
            cd = jnp.where(hit, NEG_INF, cd)
            sel = jnp.where(hit, 1.0, sel)
        cand_scr[h] = cd
        sel_scr[h] = sel

    for h in range(PEER_HEADS):
        l2_reset(h)

    def l2_fast(r, c):
        for h in range(PEER_HEADS):
            l2_round(h, False, unroll=L1_UNROLL)
        return c

    lax.fori_loop(0, PEER_TOPK // L1_UNROLL, l2_fast, 0)

    def l2_redo(h):
        l2_reset(h)

        def l2_exact(r, c):
            l2_round(h, True)
            return c

        lax.fori_loop(0, PEER_TOPK, l2_exact, 0)

    redo_where_miscounted([miscount(sel_scr[h]) for h in range(PEER_HEADS)], l2_redo)

    row8 = lax.broadcasted_iota(jnp.int32, (8, tt), 0)
    for h in range(PEER_HEADS):
        tv1 = tv_scr[2 * h]
        tv2 = tv_scr[2 * h + 1]
        sel = sel_scr[h]
        x1 = jnp.exp(tv1 - tv1[0:1, :])
        x2 = jnp.exp(tv2 - tv2[0:1, :])
        eparts = [x1 * x2[0:1, :]]
        for b in range(1, 8):
            eparts.append(x1[0:8, :] * x2[b:b + 1, :])
        eparts.append(x1[0:1, :] * x2[8:16, :])
        z = jnp.sum(sel * jnp.concatenate(eparts, axis=0), axis=0, keepdims=True)
        n_lo = sel[0:8, :]
        for b in range(1, 8):
            n_lo = n_lo + sel[16 + 8 * (b - 1):24 + 8 * (b - 1), :]
        n_lo = n_lo + jnp.where(row8 == 0, jnp.sum(sel[72:80, :], axis=0, keepdims=True), 0.0)
        s1 = lax.dot_general(sk_ref[2 * h], q_ref[:, (2 * h) * PEER_HALF:(2 * h + 1) * PEER_HALF],
                             _NT, preferred_element_type=F32)
        s2 = lax.dot_general(sk_ref[2 * h + 1],
                             q_ref[:, (2 * h + 1) * PEER_HALF:(2 * h + 2) * PEER_HALF],
                             _NT, preferred_element_type=F32)
        rank1 = rk_scr[2 * h]
        r2 = rk_scr[2 * h + 1]
        a0 = jnp.sum(sel[0:PEER_TOPK, :], axis=0, keepdims=True)
        ns = jnp.where(rank1 < a0, 1.0, 0.0)
        for a in range(8):
            ns = jnp.where(rank1 == float(a), n_lo[a:a + 1, :], ns)
        ns_ref[h] = _dup_words(ns)
        e1_ref[h] = _dup_words(jnp.exp(s1 - tv1[0:1, :]) / z)
        for w, tab in enumerate((r2, jnp.exp(s2 - tv2[0:1, :]))):
            words = _pair_words(tab, stage)
            for k in range(PEER_KEYS // 16):
                gt_ref[_gt_row(k, h, w):_gt_row(k, h, w) + 8, :] = words[8 * k:8 * k + 8, :]


GT_ROWS = (PEER_KEYS // 16) * PEER_HEADS * 2 * 8


def _gt_row(k, h, w):
    return ((k * PEER_HEADS + h) * 2 + w) * 8


def peer_topk(q, sub_keys):
    t = q.shape[0]
    tt = LANES
    nl = 2 * PEER_HEADS
    flat, neg = _cand_tables()
    out_w = jax.ShapeDtypeStruct((PEER_HEADS, PEER_KEYS, t), jnp.uint32)
    out_g = jax.ShapeDtypeStruct((t // tt * GT_ROWS, tt), jnp.uint32)
    ospec = pl.BlockSpec((PEER_HEADS, PEER_KEYS, tt), lambda i: (0, 0, i))
    return pl.pallas_call(
        _topk_kernel,
        grid=(t // tt,),
        in_specs=[pl.BlockSpec((tt, nl * PEER_HALF), lambda i: (i, 0)),
                  pl.BlockSpec((nl, PEER_KEYS, PEER_HALF), lambda i: (0, 0, 0)),
                  pl.BlockSpec((80, tt), lambda i: (0, 0)),
                  pl.BlockSpec((80, tt), lambda i: (0, 0))],
        out_specs=[ospec, ospec, pl.BlockSpec((GT_ROWS, tt), lambda i: (i, 0))],
        out_shape=[out_w, out_w, out_g],
        scratch_shapes=[pltpu.VMEM((nl, PEER_KEYS, tt), F32),
                        pltpu.VMEM((nl, PEER_KEYS, tt), F32),
                        pltpu.VMEM((nl, PEER_TOPK, tt), F32),
                        pltpu.VMEM((PEER_HEADS, 80, tt), F32),
                        pltpu.VMEM((PEER_HEADS, 80, tt), F32),
                        pltpu.VMEM((PEER_KEYS, tt), F32)],
        compiler_params=_params("parallel"),
        name="peer_topk",
    )(q, sub_keys, flat, neg)


GELU_C = float(np.sqrt(2.0 / np.pi))
L1_UNROLL = 2


def _peer_kernel(xn_ref, u_ref, vt_ref, ns_ref, e1_ref, gt_ref, h_ref, o_ref,
                 acc_ref, hta_ref, htb_ref, cfa_ref, cfb_ref, *, et, tt):
    j = pl.program_id(1)
    rows = et // PEER_KEYS
    mb = 2 * LANES
    d = xn_ref.shape[1]

    @pl.when(j == 0)
    def _():
        acc_ref[...] = jnp.zeros_like(acc_ref)

    pk = 16
    zero = jnp.zeros((pk, LANES), BF16)

    def bcast(words, r):
        return pltpu.bitcast(jnp.broadcast_to(words[r:r + 1, :], (pk // 2, LANES)), BF16)

    pieces = [(rb, nb) for rb in range(et // mb) for nb in range(tt // mb)]

    def first_piece(tile, ht_ref, p):
        rb, nb = pieces[p]
        ht_ref[rb * mb:(rb + 1) * mb, nb * mb:(nb + 1) * mb] = lax.dot_general(
            u_ref[tile * et + rb * mb:tile * et + (rb + 1) * mb, :],
            xn_ref[nb * mb:(nb + 1) * mb, :], _NT, preferred_element_type=F32)

    def second_piece(tile, cf_ref, p):
        rb, nb = pieces[p]
        acc_ref[rb * mb:(rb + 1) * mb, nb * mb:(nb + 1) * mb] += jnp.dot(
            vt_ref[tile, rb * mb:(rb + 1) * mb, :], cf_ref[:, nb * mb:(nb + 1) * mb],
            preferred_element_type=F32)

    def gate_row(tile, ht_ref, cf_ref, r):
        base = pl.multiple_of((2 * j + tile) * rows, rows)
        for c in range(tt // LANES):
            cs = slice(c * LANES, (c + 1) * LANES)
            ns = [bcast(ns_ref[h, pl.ds(base, rows), cs], r) for h in range(PEER_HEADS)]
            e1 = [bcast(e1_ref[h, pl.ds(base, rows), cs], r) for h in range(PEER_HEADS)]
            for k in range(PEER_KEYS // pk):
                g = zero
                for h in range(PEER_HEADS):
                    r2 = pltpu.bitcast(gt_ref[pl.ds(c * GT_ROWS + _gt_row(k, h, 0), 8), :], BF16)
                    e2 = pltpu.bitcast(gt_ref[pl.ds(c * GT_ROWS + _gt_row(k, h, 1), 8), :], BF16)
                    g = g + jnp.where(r2 < ns[h], e2, zero) * e1[h]
                es = slice(r * PEER_KEYS + k * pk, r * PEER_KEYS + (k + 1) * pk)
                x = ht_ref[es, cs]
                xb = x.astype(BF16)
                t = jnp.tanh(xb * (GELU_C + (GELU_C * 0.044715) * (xb * xb)))
                hx = xb * 0.5
                cf_ref[es, cs] = g * (hx + hx * t)

    npiece = len(pieces)
    assert npiece == rows and et == d
    for p in range(npiece):
        first_piece(0, hta_ref, p)
    for r in range(rows):
        gate_row(0, hta_ref, cfa_ref, r)
        first_piece(1, htb_ref, r)
    for r in range(rows):
        second_piece(0, cfa_ref, r)
        gate_row(1, htb_ref, cfb_ref, r)
    for p in range(npiece):
        second_piece(1, cfb_ref, p)

    @pl.when(j == pl.num_programs(1) - 1)
    def _():
        o_ref[...] = h_ref[...] + acc_ref[...].T


def peer_mix(xn, u, vt, ns, e1, gt, h, et=PEER_ET, tt=512):
    t, d = h.shape
    ne = u.shape[0]
    tt = min(tt, t)
    assert et % (8 * PEER_KEYS) == 0 and ne % (2 * et) == 0 and t % tt == 0
    once = pl.Buffered(1)
    gspec = pl.BlockSpec((PEER_HEADS, PEER_KEYS, tt), lambda i, j: (0, 0, i), pipeline_mode=once)
    return pl.pallas_call(
        functools.partial(_peer_kernel, et=et, tt=tt),
        grid=(t // tt, ne // (2 * et)),
        in_specs=[pl.BlockSpec((tt, d), lambda i, j: (i, 0), pipeline_mode=once),
                  pl.BlockSpec((2 * et, d), lambda i, j: (j, 0)),
                  pl.BlockSpec((2, d, et), lambda i, j: (j, 0, 0)),
                  gspec, gspec,
                  pl.BlockSpec((tt // LANES * GT_ROWS, LANES), lambda i, j: (i, 0), pipeline_mode=once),
                  pl.BlockSpec((tt, d), lambda i, j: (i, 0), pipeline_mode=once)],
        out_specs=pl.BlockSpec((tt, d), lambda i, j: (i, 0)),
        out_shape=jax.ShapeDtypeStruct((t, d), F32),
        scratch_shapes=[pltpu.VMEM((d, tt), F32),
                        pltpu.VMEM((et, tt), F32), pltpu.VMEM((et, tt), F32),
                        pltpu.VMEM((et, tt), BF16), pltpu.VMEM((et, tt), BF16)],
        compiler_params=_params("parallel", "arbitrary"),
        name="peer_mix",
    )(xn, u, vt, ns, e1, gt, h)


def _vt_tiles(v, et=PEER_ET):
    ne, d = v.shape
    return v.astype(BF16).reshape(ne // et, et, d).transpose(0, 2, 1)


def _peer_block(o, wo, h, g, wq, sub_keys, u, v):
    hn, xn, q = proj_residual_query(o, wo.astype(BF16), h, g, wq.astype(BF16))
    sk = sub_keys.reshape(2 * PEER_HEADS, PEER_KEYS, PEER_HALF).astype(BF16)
    ns, e1, gt = peer_topk(q, sk)
    return peer_mix(xn, u.astype(BF16), _vt_tiles(v), ns, e1, gt, hn)


def kernel(x, sb_norm, sb_w_qkv, sb_w_o, sw_norm, sw_w_qkv, sw_q_gain, sw_k_gain, sw_sinks, sw_w_o,
           ffn_norm, peer_w_query, peer_sub_keys, peer_u, peer_v):
    batch, seq, d = x.shape
    t = batch * seq
    h = x.reshape(t, d)

    qkv = norm_matmul(h, sb_norm[0], sb_w_qkv[0].astype(BF16), BF16)
    o = sb_attention(qkv.reshape(batch, seq, -1), batch, seq).reshape(t, -1)
    h = _peer_block(o, sb_w_o[0], h, ffn_norm[0], peer_w_query[0], peer_sub_keys[0],
                    peer_u[0], peer_v[0])

    qkv = norm_matmul(h, sw_norm[0], _sw_qkv_weight(sw_w_qkv[0]).astype(BF16), F32)
    o = sw_attention(qkv.reshape(batch, seq, -1), sw_sinks[0], sw_q_gain[0], sw_k_gain[0],
                     batch, seq).reshape(t, -1)
    h = _peer_block(o, sw_w_o[0], h, ffn_norm[1], peer_w_query[1], peer_sub_keys[1],
                    peer_u[1], peer_v[1])
    return h.reshape(batch, seq, d)
```
